```python
import math
import jax, jax.numpy as jnp
from jax import lax
import numpy as np

D_MODEL = 2048
BATCH = 8
SEQ = 2048
DEPTH = 2
DEC_BATCH = 32
DEC_SEQ = 32
PAST_LEN = 4096

CHUNK = 64
WINDOW = 128
WINDOW_CHUNKS = WINDOW // CHUNK
N_MIXERS = 2
N_ATT_LAYERS = (DEPTH + 1) // 2
N_RWKV_LAYERS = DEPTH // 2
HEAD_DIM = 64
N_HEADS = D_MODEL // HEAD_DIM
N_KV_HEADS = 4
GROUP = N_HEADS // N_KV_HEADS
Q_DIM = N_HEADS * HEAD_DIM
KV_DIM = N_KV_HEADS * HEAD_DIM
QKV_DIM = Q_DIM + 2 * KV_DIM
N_BUCKETS = 32
MAX_DISTANCE = 128
RWKV_HEAD = 64
RWKV_HEADS = D_MODEL // RWKV_HEAD
DECAY_LORA = 96
AAA_LORA = 96
GATE_LORA = 256
D_FF = 5632
RMS_EPS = 1e-6
GN_EPS = RWKV_HEAD * 1e-5
NEG_INF = -1e30

kernel_name = "hybrid_swa_sink_rwkv7_macaron_stream_step"


def rms_norm(x, g):
    xf = x.astype(jnp.float32)
    y = xf * lax.rsqrt(jnp.mean(jnp.square(xf), axis=-1, keepdims=True) + RMS_EPS)
    return (y * g.astype(jnp.float32)).astype(x.dtype)


def swiglu(h, w_gate, w_up, w_down):
    return (jax.nn.silu(h @ w_gate) * (h @ w_up)) @ w_down


def t5_bucket(rel):
    nb = N_BUCKETS // 2
    max_exact = nb // 2
    offset = jnp.where(rel > 0, nb, 0)
    n = jnp.abs(rel)
    nf = jnp.maximum(n, 1).astype(jnp.float32)
    large = max_exact + (jnp.log(nf / max_exact) / math.log(MAX_DISTANCE / max_exact)
                         * (nb - max_exact)).astype(jnp.int32)
    large = jnp.minimum(large, nb - 1)
    return offset + jnp.where(n < max_exact, n, large)


def rel_bias(table, n_q, n_k):
    rel = (jnp.arange(n_k, dtype=jnp.int32)[None, :] - WINDOW
           - jnp.arange(n_q, dtype=jnp.int32)[:, None])
    b = table[t5_bucket(rel)]
    return jnp.transpose(b, (2, 0, 1)).reshape(N_KV_HEADS, GROUP, n_q, n_k)


def band_attention(q, k, v, bias, mask, sinks):
    s = jnp.einsum('bcqhgd,bcshd->bchgqs', q, k).astype(jnp.float32) * (HEAD_DIM ** -0.5)
    s = s + bias.astype(jnp.float32)[None, None]
    s = jnp.where(mask[None, :, None, None, None, :], s, NEG_INF)
    sink = sinks.astype(jnp.float32).reshape(1, 1, N_KV_HEADS, GROUP, 1, 1)
    m = jnp.maximum(jnp.max(s, axis=-1, keepdims=True), sink)
    p = jnp.exp(s - m)
    denom = jnp.sum(p, axis=-1, keepdims=True) + jnp.exp(sink - m)
    return jnp.einsum('bchgqs,bcshd->bcqhgd', (p / denom).astype(v.dtype), v)


def attention_mixer(h, w_qkv, b_qkv, w_o, sinks, table, cache_k, cache_v):
    B, T, _ = h.shape
    qkv = h @ w_qkv + b_qkv
    q = qkv[..., :Q_DIM].reshape(B, T, N_KV_HEADS, GROUP, HEAD_DIM)
    k = qkv[..., Q_DIM:Q_DIM + KV_DIM].reshape(B, T, N_KV_HEADS, HEAD_DIM)
    v = qkv[..., Q_DIM + KV_DIM:].reshape(B, T, N_KV_HEADS, HEAD_DIM)
    if cache_k is None:
        n_c = T // CHUNK
        n_k = (WINDOW_CHUNKS + 1) * CHUNK
        qc = q.reshape(B, n_c, CHUNK, N_KV_HEADS, GROUP, HEAD_DIM)

        def band(t):
            tc = t.reshape(B, n_c, CHUNK, N_KV_HEADS, HEAD_DIM)
            tp = jnp.pad(tc, ((0, 0), (WINDOW_CHUNKS, 0), (0, 0), (0, 0), (0, 0)))
            return jnp.concatenate([tp[:, j:j + n_c] for j in range(WINDOW_CHUNKS + 1)], axis=2)

        key_chunk = (jnp.arange(n_c)[:, None] + jnp.arange(n_k)[None, :] // CHUNK - WINDOW_CHUNKS)
        mask = key_chunk >= 0
        bias = rel_bias(table, CHUNK, n_k)
        o = band_attention(qc, band(k), band(v), bias, mask, sinks).reshape(B, T, Q_DIM)
        new_k, new_v = k[:, T - WINDOW:], v[:, T - WINDOW:]
    else:
        n_k = WINDOW + T
        kb = jnp.concatenate([cache_k.astype(k.dtype), k], axis=1)[:, None]
        vb = jnp.concatenate([cache_v.astype(v.dtype), v], axis=1)[:, None]
        mask = jnp.ones((1, n_k), dtype=bool)
        bias = rel_bias(table, T, n_k)
        o = band_attention(q[:, None], kb, vb, bias, mask, sinks).reshape(B, T, Q_DIM)
        new_k, new_v = k, v
    return o @ w_o, new_k, new_v


def rwkv7_mixer(h, shift_prev, wkv_prev, mu, w_r, w_k, w_v, w_o, w0, w1, w2,
                a0, a1, a2, g1, g2, k_k, k_a, r_k, ln_w, ln_b):
    B, T, D = h.shape
    f32 = jnp.float32
    dx = jnp.concatenate([shift_prev.astype(h.dtype), h[:, :-1]], axis=1) - h
    xr, xw, xk, xv, xa, xg = [h + dx * mu[n] for n in range(6)]
    r = xr @ w_r
    k = xk @ w_k
    v = xv @ w_v
    w_log = -jax.nn.softplus(-(w0 + jnp.tanh(xw @ w1) @ w2)) - 0.5
    decay = jnp.exp(-jnp.exp(w_log.astype(f32)))
    a = jax.nn.sigmoid(a0 + (xa @ a1) @ a2)
    g = jax.nn.sigmoid(xg @ g1) @ g2

    def heads(t):
        return t.reshape(B, T, RWKV_HEADS, RWKV_HEAD).astype(f32)

    kk = heads(k * k_k)
    kk = kk / jnp.maximum(jnp.sqrt(jnp.sum(kk * kk, axis=-1, keepdims=True)), 1e-12)
    k = k * (1 + (a - 1) * k_a)
    rh, kh, vh, ah, wh = heads(r), heads(k), heads(v), heads(a), heads(decay)

    def step(S, inp):
        r_t, w_t, k_t, v_t, kk_t, a_t = inp
        sa = jnp.einsum('bhvk,bhk->bhv', S, -kk_t)
        S = (S * w_t[:, :, None, :] + sa[..., None] * (kk_t * a_t)[:, :, None, :]
             + v_t[..., None] * k_t[:, :, None, :])
        return S, jnp.einsum('bhvk,bhk->bhv', S, r_t)

    tm = lambda t: jnp.swapaxes(t, 0, 1)
    S_final, y = lax.scan(step, wkv_prev.astype(f32),
                          (tm(rh), tm(wh), tm(kh), tm(vh), tm(kk), tm(ah)))
    y = jnp.swapaxes(y, 0, 1)
    mean = jnp.mean(y, axis=-1, keepdims=True)
    var = jnp.mean(jnp.square(y - mean), axis=-1, keepdims=True)
    yn = ((y - mean) * lax.rsqrt(var + GN_EPS)).reshape(B, T, D)
    yn = yn * ln_w.astype(f32) + ln_b.astype(f32)
    bonus = jnp.sum(rh * kh * r_k.astype(f32), axis=-1, keepdims=True) * vh
    out = ((yn + bonus.reshape(B, T, D)) * g.astype(f32)).astype(h.dtype) @ w_o
    return out, h[:, T - 1:], S_final.astype(h.dtype)


def setup_inputs(seed: int = 0) -> dict:
    key = jax.random.key(seed)
    ks = iter(jax.random.split(key, 48))
    f32 = jnp.float32
    D = D_MODEL

    def nrm(shape, scale):
        return jax.random.normal(next(ks), shape, f32) * scale

    return {
        "x_prompt": nrm((BATCH, SEQ, D), 1.0),
        "x_sample": nrm((DEC_BATCH, DEC_SEQ, D), 1.0),
        "cache_k": nrm((N_ATT_LAYERS, DEC_BATCH, WINDOW, N_KV_HEADS, HEAD_DIM), 1.0),
        "cache_v": nrm((N_ATT_LAYERS, DEC_BATCH, WINDOW, N_KV_HEADS, HEAD_DIM), 1.0),
        "state_shift": nrm((N_RWKV_LAYERS, DEC_BATCH, 1, D), 1.0),
        "state_wkv": nrm((N_RWKV_LAYERS, DEC_BATCH, RWKV_HEADS, RWKV_HEAD, RWKV_HEAD), 1.0),
        "norm_g": 1.0 + nrm((DEPTH, 6, D), 0.05),
        "ffn_w_gate": nrm((DEPTH, 2, D, D_FF), D ** -0.5),
        "ffn_w_up": nrm((DEPTH, 2, D, D_FF), D ** -0.5),
        "ffn_w_down": nrm((DEPTH, 2, D_FF, D), D_FF ** -0.5),
        "rel_table": nrm((N_BUCKETS, N_HEADS), 0.5),
        "att_w_qkv": nrm((N_ATT_LAYERS, D, QKV_DIM), D ** -0.5),
        "att_b_qkv": nrm((N_ATT_LAYERS, QKV_DIM), 0.02),
        "att_w_o": nrm((N_ATT_LAYERS, Q_DIM, D), Q_DIM ** -0.5),
        "att_sinks": nrm((N_ATT_LAYERS, N_HEADS), 1.0),
        "rwkv_mu": jax.random.uniform(next(ks), (N_RWKV_LAYERS, 6, D), f32),
        "rwkv_w_r": nrm((N_RWKV_LAYERS, D, D), D ** -0.5),
        "rwkv_w_k": nrm((N_RWKV_LAYERS, D, D), D ** -0.5),
        "rwkv_w_v": nrm((N_RWKV_LAYERS, D, D), D ** -0.5),
        "rwkv_w_o": nrm((N_RWKV_LAYERS, D, D), D ** -0.5),
        "rwkv_w0": jnp.linspace(-6.5, -1.5, D, dtype=f32)[None, :] + nrm((N_RWKV_LAYERS, D), 0.1),
        "rwkv_w1": nrm((N_RWKV_LAYERS, D, DECAY_LORA), D ** -0.5),
        "rwkv_w2": nrm((N_RWKV_LAYERS, DECAY_LORA, D), 0.5 * DECAY_LORA ** -0.5),
        "rwkv_a0": nrm((N_RWKV_LAYERS, D), 0.1),
        "rwkv_a1": nrm((N_RWKV_LAYERS, D, AAA_LORA), D ** -0.5),
        "rwkv_a2": nrm((N_RWKV_LAYERS, AAA_LORA, D), AAA_LORA ** -0.5),
        "rwkv_g1": nrm((N_RWKV_LAYERS, D, GATE_LORA), D ** -0.5),
        "rwkv_g2": nrm((N_RWKV_LAYERS, GATE_LORA, D), GATE_LORA ** -0.5),
        "rwkv_k_k": 0.85 + nrm((N_RWKV_LAYERS, D), 0.05),
        "rwkv_k_a": 1.0 + nrm((N_RWKV_LAYERS, D), 0.05),
        "rwkv_r_k": nrm((N_RWKV_LAYERS, RWKV_HEADS, RWKV_HEAD), 0.1),
        "rwkv_ln_w": 1.0 + nrm((N_RWKV_LAYERS, D), 0.05),
        "rwkv_ln_b": nrm((N_RWKV_LAYERS, D), 0.02),
    }


def reference(x_prompt, x_sample, cache_k, cache_v, state_shift, state_wkv,
              norm_g, ffn_w_gate, ffn_w_up, ffn_w_down, rel_table,
              att_w_qkv, att_b_qkv, att_w_o, att_sinks,
              rwkv_mu, rwkv_w_r, rwkv_w_k, rwkv_w_v, rwkv_w_o,
              rwkv_w0, rwkv_w1, rwkv_w2, rwkv_a0, rwkv_a1, rwkv_a2,
              rwkv_g1, rwkv_g2, rwkv_k_k, rwkv_k_a, rwkv_r_k, rwkv_ln_w, rwkv_ln_b):

    def trunk(x, kv_cache, rwkv_state):
        B = x.shape[0]
        new_k, new_v, new_shift, new_wkv = [], [], [], []
        for i in range(DEPTH):
            g = norm_g[i]
            x = x + 0.5 * rms_norm(swiglu(rms_norm(x, g[0]), ffn_w_gate[i, 0],
                                          ffn_w_up[i, 0], ffn_w_down[i, 0]), g[1])
            h = rms_norm(x, g[2])
            j = i // N_MIXERS
            if i % N_MIXERS == 0:
                ck = None if kv_cache is None else kv_cache[0][j]
                cv = None if kv_cache is None else kv_cache[1][j]
                out, k_rows, v_rows = attention_mixer(h, att_w_qkv[j], att_b_qkv[j], att_w_o[j],
                                                      att_sinks[j], rel_table, ck, cv)
                new_k.append(k_rows)
                new_v.append(v_rows)
            else:
                if rwkv_state is None:
                    sp = jnp.zeros((B, 1, D_MODEL), h.dtype)
                    s0 = jnp.zeros((B, RWKV_HEADS, RWKV_HEAD, RWKV_HEAD), jnp.float32)
                else:
                    sp, s0 = rwkv_state[0][j], rwkv_state[1][j]
                out, sh, st = rwkv7_mixer(h, sp, s0, rwkv_mu[j], rwkv_w_r[j], rwkv_w_k[j],
                                          rwkv_w_v[j], rwkv_w_o[j], rwkv_w0[j], rwkv_w1[j],
                                          rwkv_w2[j], rwkv_a0[j], rwkv_a1[j], rwkv_a2[j],
                                          rwkv_g1[j], rwkv_g2[j], rwkv_k_k[j], rwkv_k_a[j],
                                          rwkv_r_k[j], rwkv_ln_w[j], rwkv_ln_b[j])
                new_shift.append(sh)
                new_wkv.append(st)
            x = x + rms_norm(out, g[3])
            x = x + 0.5 * rms_norm(swiglu(rms_norm(x, g[4]), ffn_w_gate[i, 1],
                                          ffn_w_up[i, 1], ffn_w_down[i, 1]), g[5])
        return x, jnp.stack(new_k), jnp.stack(new_v), jnp.stack(new_shift), jnp.stack(new_wkv)

    y_prompt, k_prompt, v_prompt, shift_prompt, wkv_prompt = trunk(x_prompt, None, None)
    y_sample, k_sample, v_sample, shift_sample, wkv_sample = trunk(
        x_sample, (cache_k, cache_v), (state_shift, state_wkv))
    return (y_prompt, y_sample, k_prompt, v_prompt, k_sample, v_sample,
            shift_prompt, wkv_prompt, shift_sample, wkv_sample)
```

```python
import functools
import math

import numpy as np
import jax
import jax.numpy as jnp
from jax import lax
from jax.experimental import pallas as pl
from jax.experimental.pallas import tpu as pltpu

F32 = jnp.float32
BF16 = jnp.bfloat16

HEAD_DIM = 64
N_KV_HEADS = 4
CHUNK = 64
WINDOW = 128
N_BUCKETS = 32
MAX_DISTANCE = 128
RWKV_HEAD = 64
RMS_EPS = 1e-6
GN_EPS = RWKV_HEAD * 1e-5
NEG_INF = -1e30
LORA_PAD = 128

LANES = 128
VMEM_LIMIT_BYTES = 56 * 1024 * 1024
ROW_TILE = 512
FF_TILE = 512
WKV_CHUNK = 64
WKV_PAIRS_PER_STEP = 4


def _params(*semantics):
    return pltpu.CompilerParams(dimension_semantics=semantics,
                                vmem_limit_bytes=VMEM_LIMIT_BYTES)


def _rms(x):
    return x * lax.rsqrt(jnp.mean(jnp.square(x), axis=-1, keepdims=True) + RMS_EPS)


def _dot(a, b):
    return jnp.dot(a, b, preferred_element_type=F32)


def _dot_nt(a, b):
    return lax.dot_general(a, b, (((1,), (1,)), ((), ())), preferred_element_type=F32)


def _ffn_kernel(x_ref, g0_ref, g1_ref, wg_ref, wu_ref, wd_ref, o_ref, h_ref, acc_ref):
    f = pl.program_id(1)

    @pl.when(f == 0)
    def _():
        h_ref[...] = (_rms(x_ref[...]) * g0_ref[...]).astype(BF16)
        acc_ref[...] = jnp.zeros_like(acc_ref)

    h = h_ref[...]
    a = _dot(h, wg_ref[...])
    b = _dot(h, wu_ref[...])
    act = (a * jax.nn.sigmoid(a) * b).astype(BF16)
    acc_ref[...] += _dot(act, wd_ref[...])

    @pl.when(f == pl.num_programs(1) - 1)
    def _():
        o_ref[...] = x_ref[...] + 0.5 * (_rms(acc_ref[...]) * g1_ref[...])


def _ffn_block(x, g0, g1, wg, wu, wd):
    n, d = x.shape
    d_ff = wg.shape[1]
    grid = (n // ROW_TILE, d_ff // FF_TILE)
    return pl.pallas_call(
        _ffn_kernel,
        grid=grid,
        in_specs=[
            pl.BlockSpec((ROW_TILE, d), lambda i, f: (i, 0)),
            pl.BlockSpec((1, d), lambda i, f: (0, 0)),
            pl.BlockSpec((1, d), lambda i, f: (0, 0)),
            pl.BlockSpec((d, FF_TILE), lambda i, f: (0, f)),
            pl.BlockSpec((d, FF_TILE), lambda i, f: (0, f)),
            pl.BlockSpec((FF_TILE, d), lambda i, f: (f, 0)),
        ],
        out_specs=pl.BlockSpec((ROW_TILE, d), lambda i, f: (i, 0)),
        out_shape=jax.ShapeDtypeStruct((n, d), F32),
        scratch_shapes=[pltpu.VMEM((ROW_TILE, d), BF16), pltpu.VMEM((ROW_TILE, d), F32)],
        compiler_params=_params("parallel", "arbitrary"),
        name="ffn_block",
    )(x, g0.reshape(1, d), g1.reshape(1, d), wg, wu, wd)


def _norm_kernel(x_ref, g_ref, o_ref):
    o_ref[...] = _rms(x_ref[...]) * g_ref[...]


def _norm(x, g):
    n, d = x.shape
    return pl.pallas_call(
        _norm_kernel,
        grid=(n // ROW_TILE,),
        in_specs=[pl.BlockSpec((ROW_TILE, d), lambda i: (i, 0)),
                  pl.BlockSpec((1, d), lambda i: (0, 0))],
        out_specs=pl.BlockSpec((ROW_TILE, d), lambda i: (i, 0)),
        out_shape=jax.ShapeDtypeStruct((n, d), F32),
        compiler_params=_params("parallel"),
        name="rms_norm",
    )(x, g.reshape(1, d))


def _qkv_kernel(x_ref, g_ref, w_ref, b_ref, o_ref, h_ref):
    @pl.when(pl.program_id(1) == 0)
    def _():
        h_ref[...] = (_rms(x_ref[...]) * g_ref[...]).astype(BF16)

    o_ref[...] = _dot(h_ref[...], w_ref[...]) + b_ref[...]


def _qkv_proj(x, g, w, b, col_tile):
    n, d = x.shape
    m = w.shape[1]
    return pl.pallas_call(
        _qkv_kernel,
        grid=(n // ROW_TILE, m // col_tile),
        in_specs=[pl.BlockSpec((ROW_TILE, d), lambda i, j: (i, 0)),
                  pl.BlockSpec((1, d), lambda i, j: (0, 0)),
                  pl.BlockSpec((d, col_tile), lambda i, j: (0, j)),
                  pl.BlockSpec((1, col_tile), lambda i, j: (0, j))],
        out_specs=pl.BlockSpec((ROW_TILE, col_tile), lambda i, j: (i, j)),
        out_shape=jax.ShapeDtypeStruct((n, m), F32),
        scratch_shapes=[pltpu.VMEM((ROW_TILE, d), BF16)],
        compiler_params=_params("parallel", "arbitrary"),
        name="qkv_proj",
    )(x, g.reshape(1, d), w, b.reshape(1, m))


def _rkv_kernel(h_ref, hp_ref, mu_ref, w_ref, o_ref, x_ref):
    @pl.when(pl.program_id(2) == 0)
    def _():
        h = h_ref[...]
        x_ref[...] = (h + (hp_ref[...] - h) * mu_ref[...]).astype(BF16)

    o_ref[...] = _dot(x_ref[...], w_ref[...])


def _rkv_proj(h, hp, mu, w, col_tile):
    n, d = h.shape
    k = w.shape[0]
    m = w.shape[2]
    return pl.pallas_call(
        _rkv_kernel,
        grid=(n // ROW_TILE, k, m // col_tile),
        in_specs=[pl.BlockSpec((ROW_TILE, d), lambda i, s, j: (i, 0)),
                  pl.BlockSpec((ROW_TILE, d), lambda i, s, j: (i, 0)),
                  pl.BlockSpec((None, 1, d), lambda i, s, j: (s, 0, 0)),
                  pl.BlockSpec((None, d, col_tile), lambda i, s, j: (s, 0, j))],
        out_specs=pl.BlockSpec((None, ROW_TILE, col_tile), lambda i, s, j: (s, i, j)),
        out_shape=jax.ShapeDtypeStruct((k, n, m), F32),
        scratch_shapes=[pltpu.VMEM((ROW_TILE, d), BF16)],
        compiler_params=_params("parallel", "arbitrary", "arbitrary"),
        name="rkv_proj",
    )(h, hp, mu.reshape(k, 1, d), w)


def _lora_kernel(h_ref, hp_ref, mu_ref, w0_ref, w1_ref, w2_ref, a0_ref, a1_ref, a2_ref,
                 g1_ref, g2_ref, lw_ref, al_ref, gate_ref):
    h = h_ref[...]
    dx = hp_ref[...] - h

    xw = (h + dx * mu_ref[0]).astype(BF16)
    t = jnp.tanh(_dot(xw, w1_ref[...])).astype(BF16)
    z = -(w0_ref[...] + _dot(t, w2_ref[...]))
    softplus = jnp.maximum(z, 0.0) + jnp.log(1.0 + jnp.exp(-jnp.abs(z)))
    lw_ref[...] = -jnp.exp(-softplus - 0.5)

    xa = (h + dx * mu_ref[1]).astype(BF16)
    u = _dot(xa, a1_ref[...]).astype(BF16)
    al_ref[...] = jax.nn.sigmoid(a0_ref[...] + _dot(u, a2_ref[...]))

    xg = (h + dx * mu_ref[2]).astype(BF16)
    s = jax.nn.sigmoid(_dot(xg, g1_ref[...])).astype(BF16)
    gate_ref[...] = _dot(s, g2_ref[...])


def _lora_branches(h, hp, mu, w0, w1, w2, a0, a1, a2, g1, g2):
    n, d = h.shape
    row = pl.BlockSpec((ROW_TILE, d), lambda i: (i, 0))

    def whole(x):
        return pl.BlockSpec(x.shape, lambda i: (0,) * x.ndim)

    mu = mu.reshape(3, 1, d)
    w0 = w0.reshape(1, d)
    a0 = a0.reshape(1, d)
    consts = (mu, w0, w1, w2, a0, a1, a2, g1, g2)
    out = jax.ShapeDtypeStruct((n, d), F32)
    return pl.pallas_call(
        _lora_kernel,
        grid=(n // ROW_TILE,),
        in_specs=[row, row] + [whole(c) for c in consts],
        out_specs=[row, row, row],
        out_shape=[out, out, out],
        compiler_params=_params("parallel"),
        name="rwkv_lora",
    )(h, hp, *consts)


def _out_proj_kernel(o_ref, w_ref, g_ref, x_ref, y_ref):
    y = _dot(o_ref[...], w_ref[...])
    y_ref[...] = x_ref[...] + _rms(y) * g_ref[...]


def _out_proj(o, w, g, xres):
    n, d = xres.shape
    k = o.shape[1]
    return pl.pallas_call(
        _out_proj_kernel,
        grid=(n // ROW_TILE,),
        in_specs=[pl.BlockSpec((ROW_TILE, k), lambda i: (i, 0)),
                  pl.BlockSpec((k, d), lambda i: (0, 0)),
                  pl.BlockSpec((1, d), lambda i: (0, 0)),
                  pl.BlockSpec((ROW_TILE, d), lambda i: (i, 0))],
        out_specs=pl.BlockSpec((ROW_TILE, d), lambda i: (i, 0)),
        out_shape=jax.ShapeDtypeStruct((n, d), F32),
        compiler_params=_params("parallel"),
        name="out_proj",
    )(o, w, g.reshape(1, d), xres)


def _t5_bucket(rel):
    nb = N_BUCKETS // 2
    max_exact = nb // 2
    offset = jnp.where(rel > 0, nb, 0)
    n = jnp.abs(rel)
    nf = jnp.maximum(n, 1).astype(F32)
    large = max_exact + (jnp.log(nf / max_exact) / math.log(MAX_DISTANCE / max_exact)
                         * (nb - max_exact)).astype(jnp.int32)
    large = jnp.minimum(large, nb - 1)
    return offset + jnp.where(n < max_exact, n, large)


def _bias_kernel(tt_ref, b_ref, o_ref):
    onehot = (lax.broadcasted_iota(jnp.int32, (N_BUCKETS, b_ref.shape[1]), 0) == b_ref[...]).astype(F32)
    o_ref[...] = jnp.dot(tt_ref[...], onehot, preferred_element_type=F32,
                         precision=lax.Precision.HIGHEST)


def _rel_bias(table, n_q, n_k):
    rel = (jnp.arange(n_k, dtype=jnp.int32)[None, :] - WINDOW - jnp.arange(n_q, dtype=jnp.int32)[:, None])
    buckets = _t5_bucket(rel).reshape(1, n_q * n_k).astype(jnp.int32)
    n_heads = table.shape[1]
    out = pl.pallas_call(
        _bias_kernel,
        out_shape=jax.ShapeDtypeStruct((n_heads, n_q * n_k), F32),
        name="rel_bias",
    )(table.T, buckets)
    return out.reshape(n_heads, n_q, n_k)


def _attn_kernel(*refs, n_parts, group, first_valid_step):
    q_ref = refs[0]
    k_refs = refs[1:1 + n_parts]
    v_refs = refs[1 + n_parts:1 + 2 * n_parts]
    bias_ref, sink_ref, o_ref = refs[1 + 2 * n_parts:]
    n_q = q_ref.shape[0]
    step = pl.program_id(1)
    scale = HEAD_DIM ** -0.5

    part_rows = [r.shape[0] for r in k_refs]
    outs = []
    for hk in range(N_KV_HEADS):
        kv_cols = slice(hk * HEAD_DIM, (hk + 1) * HEAD_DIM)
        k = jnp.concatenate([r[:, kv_cols] for r in k_refs], axis=0).astype(BF16)
        v = jnp.concatenate([r[:, kv_cols] for r in v_refs], axis=0).astype(BF16)
        q = jnp.concatenate(
            [q_ref[:, (hk * group + g) * HEAD_DIM:(hk * group + g + 1) * HEAD_DIM] for g in range(group)],
            axis=0).astype(BF16)
        s = _dot_nt(q, k) * scale
        s = s + bias_ref[hk * group:(hk + 1) * group].reshape(group * n_q, -1)
        col = lax.broadcasted_iota(jnp.int32, s.shape, 1)
        start = 0
        for p, rows in enumerate(part_rows):
            if first_valid_step[p] > 0:
                hidden = (col >= start) & (col < start + rows) & (step < first_valid_step[p])
                s = jnp.where(hidden, NEG_INF, s)
            start += rows
        sink = sink_ref[hk * group * n_q:(hk + 1) * group * n_q]
        m = jnp.maximum(jnp.max(s, axis=-1, keepdims=True), sink)
        p_ = jnp.exp(s - m)
        denom = jnp.sum(p_, axis=-1, keepdims=True) + jnp.exp(sink - m)
        o = _dot((p_ / denom).astype(BF16), v)
        outs.extend(o[g * n_q:(g + 1) * n_q] for g in range(group))
    o_ref[...] = jnp.concatenate(outs, axis=1).astype(o_ref.dtype)


def _attention(q, k_parts, v_parts, part_index_maps, first_valid_step, bias, sinks, n_steps, q_rows):
    n_b, _, q_dim = q.shape
    n_heads = q_dim // HEAD_DIM
    group = n_heads // N_KV_HEADS
    n_parts = len(k_parts)
    kv_dim = N_KV_HEADS * HEAD_DIM
    part_rows = [rows for rows, _ in part_index_maps]
    n_k = sum(part_rows)
    assert bias.shape == (n_heads, q_rows, n_k)
    sink_col = jnp.repeat(sinks.astype(F32), q_rows).reshape(n_heads * q_rows, 1)
    kv_specs = [pl.BlockSpec((None, rows, kv_dim), imap) for rows, imap in part_index_maps]
    return pl.pallas_call(
        functools.partial(_attn_kernel, n_parts=n_parts, group=group,
                          first_valid_step=tuple(first_valid_step)),
        grid=(n_b, n_steps),
        in_specs=[pl.BlockSpec((None, q_rows, q_dim), lambda b, c: (b, c, 0))] + kv_specs + kv_specs + [
            pl.BlockSpec(bias.shape, lambda b, c: (0, 0, 0)),
            pl.BlockSpec(sink_col.shape, lambda b, c: (0, 0))],
        out_specs=pl.BlockSpec((None, q_rows, q_dim), lambda b, c: (b, c, 0)),
        out_shape=jax.ShapeDtypeStruct(q.shape, BF16),
        compiler_params=_params("parallel", "arbitrary"),
        name="swa_attention",
    )(q, *k_parts, *v_parts, bias, sink_col)


def _split3(x):
    hi = x.astype(BF16)
    r1 = x - hi.astype(F32)
    mid = r1.astype(BF16)
    lo = (r1 - mid.astype(F32)).astype(BF16)
    return hi, mid, lo


def _wkv_kernel(r_ref, k_ref, v_ref, lw_ref, al_ref, gate_ref, kk_ref, ka_ref, rk_ref, lnw_ref,
                lnb_ref, s0_ref, o_ref, s_out_ref, h_ref):
    L = r_ref.shape[0]
    L2 = 2 * L
    c_idx = pl.program_id(2)
    half = RWKV_HEAD

    lane = lax.broadcasted_iota(jnp.int32, (1, LANES), 1)
    head0 = lane < half
    row2 = lax.broadcasted_iota(jnp.int32, (L2, L2), 0)
    col2 = lax.broadcasted_iota(jnp.int32, (L2, L2), 1)
    strict_lower = row2 > col2
    lower = row2 >= col2
    eye2 = (row2 == col2).astype(F32)
    tril = (lax.broadcasted_iota(jnp.int32, (L, L), 0) >= lax.broadcasted_iota(jnp.int32, (L, L), 1)
            ).astype(BF16)
    krow = lax.broadcasted_iota(jnp.int32, (LANES, LANES), 0)
    kcol = lax.broadcasted_iota(jnp.int32, (LANES, LANES), 1)
    same_head = (krow < half) == (kcol < half)
    eye_k = (krow == kcol).astype(F32)

    def seg_sum(x):
        s0 = jnp.sum(jnp.where(head0, x, 0.0), axis=-1, keepdims=True)
        s1 = jnp.sum(jnp.where(head0, 0.0, x), axis=-1, keepdims=True)
        return jnp.where(head0, s0, s1)

    def stack2(x):
        return jnp.concatenate([jnp.where(head0, x, 0.0), jnp.where(head0, 0.0, x)], axis=0)

    def fold(m):
        return m[:L] + m[L:]

    for p in range(WKV_PAIRS_PER_STEP):
        cols = slice(p * LANES, (p + 1) * LANES)

        @pl.when(c_idx == 0)
        def _():
            s0 = s0_ref[p]
            h_ref[p] = jnp.where(same_head, jnp.concatenate([s0, s0], axis=1), 0.0)

        r = r_ref[:, cols]
        k_raw = k_ref[:, cols]
        v = v_ref[:, cols]
        lw = lw_ref[:, cols]
        al = al_ref[:, cols]

        kk = k_raw * kk_ref[:, cols]
        kk = kk / jnp.maximum(jnp.sqrt(seg_sum(kk * kk)), 1e-12)
        k = k_raw * (1.0 + (al - 1.0) * ka_ref[:, cols])
        a = -kk
        b = kk * al

        hi, mid, lo = _split3(lw)
        c3 = _dot(tril, jnp.concatenate([hi, mid, lo], axis=1))
        c = c3[:, :LANES] + c3[:, LANES:2 * LANES] + c3[:, 2 * LANES:]
        c_last = c[L - 1:L]
        e_pos = jnp.exp(c)
        e_neg = jnp.exp(-c)
        e_end = jnp.exp(c_last - c)
        at = a * jnp.exp(c - lw)
        bt = b * e_neg
        kt = k * e_neg
        rt = r * e_pos
        bc = b * e_end
        kc = k * e_end
        d_end = jnp.exp(c_last)

        at2 = stack2(at)
        v2 = stack2(v)
        gram = _dot_nt(jnp.concatenate([at2, stack2(rt)], axis=0).astype(BF16),
                       jnp.concatenate([stack2(bt), stack2(kt)], axis=0).astype(BF16))
        n_mat = jnp.where(strict_lower, gram[:L2, :L2], 0.0)
        a_ak = jnp.where(strict_lower, gram[:L2, L2:], 0.0)
        a_rb = jnp.where(lower, gram[L2:, :L2], 0.0)
        a_rk = jnp.where(lower, gram[L2:, L2:], 0.0)

        size = 1
        t_inv = eye2
        while size < L:
            quadrant = ((row2 // (2 * size)) == (col2 // (2 * size))) & \
                       ((row2 // size) % 2 == 1) & ((col2 // size) % 2 == 0)
            n_c = jnp.where(quadrant, n_mat, 0.0)
            if size == 1:
                t_inv = t_inv + n_c
            else:
                t_bf = t_inv.astype(BF16)
                t_inv = t_inv + _dot(_dot(t_bf, n_c.astype(BF16)).astype(BF16), t_bf)
            size *= 2

        akv2 = _dot(a_ak.astype(BF16), v2.astype(BF16))
        u2 = _dot(t_inv.astype(BF16),
                  jnp.concatenate([at2, akv2], axis=1).astype(BF16))
        z = jnp.concatenate(
            [u2, jnp.concatenate([jnp.zeros((L2, LANES), F32), v2], axis=1)], axis=0).astype(BF16)
        lhs_y = jnp.concatenate([fold(a_rb), fold(a_rk)], axis=1)
        lhs_h = jnp.concatenate([stack2(bc), stack2(kc)], axis=0).T
        yz = _dot(jnp.concatenate([lhs_y, lhs_h], axis=0).astype(BF16), z)
        q_hat = rt + yz[:L, :LANES]
        y_intra = yz[:L, LANES:]
        m_mat = eye_k * d_end + yz[L:, :LANES]
        c_mat = yz[L:, LANES:]

        h_prev = h_ref[p]
        hh = _dot(jnp.concatenate([m_mat, q_hat], axis=0).astype(BF16), h_prev.astype(BF16))
        h_new = jnp.where(same_head, hh[:LANES] + c_mat, 0.0)
        h_ref[p] = h_new
        y = hh[LANES:] + y_intra

        mean = seg_sum(y) * (1.0 / half)
        yc = y - mean
        var = seg_sum(yc * yc) * (1.0 / half)
        yn = yc * lax.rsqrt(var + GN_EPS) * lnw_ref[:, cols] + lnb_ref[:, cols]
        bonus = seg_sum(r * k * rk_ref[:, cols]) * v
        o_ref[:, cols] = ((yn + bonus) * gate_ref[:, cols]).astype(o_ref.dtype)

        @pl.when(c_idx == pl.num_programs(2) - 1)
        def _():
            s_out_ref[p] = h_new[:, :half] + h_new[:, half:]


def _wkv(r, k, v, lw, al, gate, k_k, k_a, r_k, ln_w, ln_b, state):
    n_b, t, d = r.shape
    n_pairs = d // LANES
    step_lanes = WKV_PAIRS_PER_STEP * LANES
    n_groups = n_pairs // WKV_PAIRS_PER_STEP
    s_in = jnp.swapaxes(state, -1, -2).reshape(n_b, n_pairs, LANES, RWKV_HEAD)
    seq = pl.BlockSpec((None, WKV_CHUNK, step_lanes), lambda b, g, c: (b, c, g))
    vec = pl.BlockSpec((1, step_lanes), lambda b, g, c: (0, g))
    st = pl.BlockSpec((None, WKV_PAIRS_PER_STEP, LANES, RWKV_HEAD), lambda b, g, c: (b, g, 0, 0))
    o, s_out = pl.pallas_call(
        _wkv_kernel,
        grid=(n_b, n_groups, t // WKV_CHUNK),
        in_specs=[seq] * 6 + [vec] * 5 + [st],
        out_specs=[seq, st],
        out_shape=[jax.ShapeDtypeStruct((n_b, t, d), BF16),
                   jax.ShapeDtypeStruct(s_in.shape, F32)],
        scratch_shapes=[pltpu.VMEM((WKV_PAIRS_PER_STEP, LANES, LANES), F32)],
        compiler_params=_params("parallel", "parallel", "arbitrary"),
        name="rwkv7_wkv",
    )(r, k, v, lw, al, gate, *[x.reshape(1, d) for x in (k_k, k_a, r_k, ln_w, ln_b)], s_in)
    s_out = jnp.swapaxes(s_out.reshape(n_b, d // RWKV_HEAD, RWKV_HEAD, RWKV_HEAD), -1, -2)
    return o, s_out


def _pad_lora(w_in, w_out):
    rank = w_in.shape[1]
    pad = (-rank) % LORA_PAD
    return (jnp.pad(w_in, ((0, 0), (0, pad))).astype(BF16),
            jnp.pad(w_out, ((0, pad), (0, 0))).astype(BF16))


def kernel(x_prompt, x_sample, cache_k, cache_v, state_shift, state_wkv, norm_g, ffn_w_gate, ffn_w_up, ffn_w_down, rel_table, att_w_qkv, att_b_qkv, att_w_o, att_sinks, rwkv_mu, rwkv_w_r, rwkv_w_k, rwkv_w_v, rwkv_w_o, rwkv_w0, rwkv_w1, rwkv_w2, rwkv_a0, rwkv_a1, rwkv_a2, rwkv_g1, rwkv_g2, rwkv_k_k, rwkv_k_a, rwkv_r_k, rwkv_ln_w, rwkv_ln_b):
    n_bp, t_p, d = x_prompt.shape
    n_bs, t_s, _ = x_sample.shape
    n_p = n_bp * t_p
    depth = norm_g.shape[0]
    q_dim = att_w_o.shape[1]
    kv_dim = N_KV_HEADS * HEAD_DIM

    x = jnp.concatenate([x_prompt.reshape(n_p, d), x_sample.reshape(n_bs * t_s, d)], axis=0)

    wg = ffn_w_gate.astype(BF16)
    wu = ffn_w_up.astype(BF16)
    wd = ffn_w_down.astype(BF16)

    new_k, new_v, new_shift, new_wkv = [], [], [], []
    for i in range(depth):
        g = norm_g[i]
        j = i // 2
        x = _ffn_block(x, g[0], g[1], wg[i, 0], wu[i, 0], wd[i, 0])
        if i % 2 == 0:
            qkv = _qkv_proj(x, g[2], att_w_qkv[j].astype(BF16), att_b_qkv[j], col_tile=512)
            qkv_p = qkv[:n_p].reshape(n_bp, t_p, -1)
            qkv_s = qkv[n_p:].reshape(n_bs, t_s, -1)
            q_p, k_p, v_p = qkv_p[..., :q_dim], qkv_p[..., q_dim:q_dim + kv_dim], qkv_p[..., q_dim + kv_dim:]
            q_s, k_s, v_s = qkv_s[..., :q_dim], qkv_s[..., q_dim:q_dim + kv_dim], qkv_s[..., q_dim + kv_dim:]

            n_back = WINDOW // CHUNK
            maps = [(CHUNK, (lambda b, c, back=back: (b, jnp.maximum(c - back, 0), 0)))
                    for back in range(n_back, -1, -1)]
            bias_p = _rel_bias(rel_table, CHUNK, (n_back + 1) * CHUNK)
            o_p = _attention(q_p, [k_p] * (n_back + 1), [v_p] * (n_back + 1), maps,
                             [back for back in range(n_back, -1, -1)], bias_p, att_sinks[j],
                             n_steps=t_p // CHUNK, q_rows=CHUNK)
            ck = cache_k[j].reshape(n_bs, WINDOW, kv_dim)
            cv = cache_v[j].reshape(n_bs, WINDOW, kv_dim)
            bias_s = _rel_bias(rel_table, t_s, WINDOW + t_s)
            maps_s = [(WINDOW, lambda b, c: (b, 0, 0)), (t_s, lambda b, c: (b, 0, 0))]
            o_s = _attention(q_s, [ck, k_s], [cv, v_s], maps_s, [0, 0], bias_s, att_sinks[j],
                             n_steps=1, q_rows=t_s)
            o = jnp.concatenate([o_p.reshape(n_p, q_dim), o_s.reshape(n_bs * t_s, q_dim)], axis=0)
            x = _out_proj(o, att_w_o[j].astype(BF16), g[3], x)
            new_k.append((k_p[:, t_p - WINDOW:].reshape(n_bp, WINDOW, N_KV_HEADS, HEAD_DIM),
                          k_s.reshape(n_bs, t_s, N_KV_HEADS, HEAD_DIM)))
            new_v.append((v_p[:, t_p - WINDOW:].reshape(n_bp, WINDOW, N_KV_HEADS, HEAD_DIM),
                          v_s.reshape(n_bs, t_s, N_KV_HEADS, HEAD_DIM)))
        else:
            h = _norm(x, g[2])
            h_p = h[:n_p].reshape(n_bp, t_p, d)
            h_s = h[n_p:].reshape(n_bs, t_s, d)
            hp = jnp.concatenate([
                jnp.concatenate([jnp.zeros((n_bp, 1, d), F32), h_p[:, :-1]], axis=1).reshape(n_p, d),
                jnp.concatenate([state_shift[j], h_s[:, :-1]], axis=1).reshape(n_bs * t_s, d)], axis=0)
            mu = rwkv_mu[j]
            rkv = _rkv_proj(h, hp, jnp.stack([mu[0], mu[2], mu[3]]),
                            jnp.stack([rwkv_w_r[j], rwkv_w_k[j], rwkv_w_v[j]]).astype(BF16), col_tile=512)
            w1, w2 = _pad_lora(rwkv_w1[j], rwkv_w2[j])
            a1, a2 = _pad_lora(rwkv_a1[j], rwkv_a2[j])
            g1, g2 = _pad_lora(rwkv_g1[j], rwkv_g2[j])
            lw, al, gate = _lora_branches(h, hp, jnp.stack([mu[1], mu[4], mu[5]]), rwkv_w0[j], w1, w2,
                                          rwkv_a0[j], a1, a2, g1, g2)
            vecs = (rwkv_k_k[j], rwkv_k_a[j], rwkv_r_k[j].reshape(d), rwkv_ln_w[j], rwkv_ln_b[j])

            def split(a):
                return a[:n_p].reshape(n_bp, t_p, d), a[n_p:].reshape(n_bs, t_s, d)

            seqs = [split(a) for a in (rkv[0], rkv[1], rkv[2], lw, al, gate)]
            zero_state = jnp.zeros((n_bp, d // RWKV_HEAD, RWKV_HEAD, RWKV_HEAD), F32)
            o_p, wkv_p = _wkv(*[s[0] for s in seqs], *vecs, zero_state)
            pad = (-t_s) % WKV_CHUNK
            o_s, wkv_s = _wkv(*[jnp.pad(s[1], ((0, 0), (0, pad), (0, 0))) for s in seqs], *vecs,
                              state_wkv[j])
            o = jnp.concatenate([o_p.reshape(n_p, d), o_s[:, :t_s].reshape(n_bs * t_s, d)], axis=0)
            x = _out_proj(o, rwkv_w_o[j].astype(BF16), g[3], x)
            new_shift.append((h_p[:, t_p - 1:], h_s[:, t_s - 1:]))
            new_wkv.append((wkv_p, wkv_s))
        x = _ffn_block(x, g[4], g[5], wg[i, 1], wu[i, 1], wd[i, 1])

    def both(pairs):
        return jnp.stack([p[0] for p in pairs]), jnp.stack([p[1] for p in pairs])

    k_prompt, k_sample = both(new_k)
    v_prompt, v_sample = both(new_v)
    shift_prompt, shift_sample = both(new_shift)
    wkv_prompt, wkv_sample = both(new_wkv)
    return (x[:n_p].reshape(n_bp, t_p, d), x[n_p:].reshape(n_bs, t_s, d),
            k_prompt, v_prompt, k_sample, v_sample,
            shift_prompt, wkv_prompt, shift_sample, wkv_sample)
```

```python
import functools
import math

import numpy as np
import jax
import jax.numpy as jnp
from jax import lax
from jax.experimental import pallas as pl
from jax.experimental.pallas import tpu as pltpu

F32 = jnp.float32
BF16 = jnp.bfloat16

HEAD_DIM = 64
N_KV_HEADS = 4
CHUNK = 64
WINDOW = 128
N_BUCKETS = 32
MAX_DISTANCE = 128
RWKV_HEAD = 64
RMS_EPS = 1e-6
GN_EPS = RWKV_HEAD * 1e-5
NEG_INF = -1e30
LORA_PAD = 128

LANES = 128
VMEM_LIMIT_BYTES = 56 * 1024 * 1024
ROW_TILE = 512
FF_TILE = 512
WKV_CHUNK = 64
WKV_PAIRS_PER_STEP = 16


def _params(*semantics):
    return pltpu.CompilerParams(dimension_semantics=semantics,
                                vmem_limit_bytes=VMEM_LIMIT_BYTES)


def _rms(x):
    return x * lax.rsqrt(jnp.mean(jnp.square(x), axis=-1, keepdims=True) + RMS_EPS)


def _dot(a, b):
    return jnp.dot(a, b, preferred_element_type=F32)


def _dot_nt(a, b):
    return lax.dot_general(a, b, (((1,), (1,)), ((), ())), preferred_element_type=F32)


def _ffn_kernel(x_ref, g0_ref, g1_ref, wg_ref, wu_ref, wd_ref, o_ref, h_ref, acc_ref):
    f = pl.program_id(1)

    @pl.when(f == 0)
    def _():
        h_ref[...] = (_rms(x_ref[...]) * g0_ref[...]).astype(BF16)
        acc_ref[...] = jnp.zeros_like(acc_ref)

    h = h_ref[...]
    a = _dot(h, wg_ref[...])
    b = _dot(h, wu_ref[...])
    act = (a * jax.nn.sigmoid(a) * b).astype(BF16)
    acc_ref[...] += _dot(act, wd_ref[...])

    @pl.when(f == pl.num_programs(1) - 1)
    def _():
        o_ref[...] = x_ref[...] + 0.5 * (_rms(acc_ref[...]) * g1_ref[...])


def _ffn_block(x, g0, g1, wg, wu, wd):
    n, d = x.shape
    d_ff = wg.shape[1]
    grid = (n // ROW_TILE, d_ff // FF_TILE)
    return pl.pallas_call(
        _ffn_kernel,
        grid=grid,
        in_specs=[
            pl.BlockSpec((ROW_TILE, d), lambda i, f: (i, 0)),
            pl.BlockSpec((1, d), lambda i, f: (0, 0)),
            pl.BlockSpec((1, d), lambda i, f: (0, 0)),
            pl.BlockSpec((d, FF_TILE), lambda i, f: (0, f)),
            pl.BlockSpec((d, FF_TILE), lambda i, f: (0, f)),
            pl.BlockSpec((FF_TILE, d), lambda i, f: (f, 0)),
        ],
        out_specs=pl.BlockSpec((ROW_TILE, d), lambda i, f: (i, 0)),
        out_shape=jax.ShapeDtypeStruct((n, d), F32),
        scratch_shapes=[pltpu.VMEM((ROW_TILE, d), BF16), pltpu.VMEM((ROW_TILE, d), F32)],
        compiler_params=_params("parallel", "arbitrary"),
        name="ffn_block",
    )(x, g0.reshape(1, d), g1.reshape(1, d), wg, wu, wd)


def _norm_kernel(x_ref, g_ref, o_ref):
    o_ref[...] = _rms(x_ref[...]) * g_ref[...]


def _norm(x, g):
    n, d = x.shape
    return pl.pallas_call(
        _norm_kernel,
        grid=(n // ROW_TILE,),
        in_specs=[pl.BlockSpec((ROW_TILE, d), lambda i: (i, 0)),
                  pl.BlockSpec((1, d), lambda i: (0, 0))],
        out_specs=pl.BlockSpec((ROW_TILE, d), lambda i: (i, 0)),
        out_shape=jax.ShapeDtypeStruct((n, d), F32),
        compiler_params=_params("parallel"),
        name="rms_norm",
    )(x, g.reshape(1, d))


def _qkv_kernel(x_ref, g_ref, w_ref, b_ref, o_ref, h_ref):
    @pl.when(pl.program_id(1) == 0)
    def _():
        h_ref[...] = (_rms(x_ref[...]) * g_ref[...]).astype(BF16)

    o_ref[...] = _dot(h_ref[...], w_ref[...]) + b_ref[...]


def _qkv_proj(x, g, w, b, col_tile):
    n, d = x.shape
    m = w.shape[1]
    return pl.pallas_call(
        _qkv_kernel,
        grid=(n // ROW_TILE, m // col_tile),
        in_specs=[pl.BlockSpec((ROW_TILE, d), lambda i, j: (i, 0)),
                  pl.BlockSpec((1, d), lambda i, j: (0, 0)),
                  pl.BlockSpec((d, col_tile), lambda i, j: (0, j)),
                  pl.BlockSpec((1, col_tile), lambda i, j: (0, j))],
        out_specs=pl.BlockSpec((ROW_TILE, col_tile), lambda i, j: (i, j)),
        out_shape=jax.ShapeDtypeStruct((n, m), F32),
        scratch_shapes=[pltpu.VMEM((ROW_TILE, d), BF16)],
        compiler_params=_params("parallel", "arbitrary"),
        name="qkv_proj",
    )(x, g.reshape(1, d), w, b.reshape(1, m))


def _rkv_kernel(h_ref, hp_ref, mu_ref, w_ref, o_ref, x_ref):
    @pl.when(pl.program_id(2) == 0)
    def _():
        h = h_ref[...]
        x_ref[...] = (h + (hp_ref[...] - h) * mu_ref[...]).astype(BF16)

    o_ref[...] = _dot(x_ref[...], w_ref[...])


def _rkv_proj(h, hp, mu, w, col_tile):
    n, d = h.shape
    k = w.shape[0]
    m = w.shape[2]
    return pl.pallas_call(
        _rkv_kernel,
        grid=(n // ROW_TILE, k, m // col_tile),
        in_specs=[pl.BlockSpec((ROW_TILE, d), lambda i, s, j: (i, 0)),
                  pl.BlockSpec((ROW_TILE, d), lambda i, s, j: (i, 0)),
                  pl.BlockSpec((None, 1, d), lambda i, s, j: (s, 0, 0)),
                  pl.BlockSpec((None, d, col_tile), lambda i, s, j: (s, 0, j))],
        out_specs=pl.BlockSpec((None, ROW_TILE, col_tile), lambda i, s, j: (s, i, j)),
        out_shape=jax.ShapeDtypeStruct((k, n, m), F32),
        scratch_shapes=[pltpu.VMEM((ROW_TILE, d), BF16)],
        compiler_params=_params("parallel", "arbitrary", "arbitrary"),
        name="rkv_proj",
    )(h, hp, mu.reshape(k, 1, d), w)


def _lora_kernel(h_ref, hp_ref, mu_ref, w0_ref, w1_ref, w2_ref, a0_ref, a1_ref, a2_ref,
                 g1_ref, g2_ref, lw_ref, al_ref, gate_ref):
    h = h_ref[...]
    dx = hp_ref[...] - h

    xw = (h + dx * mu_ref[0]).astype(BF16)
    t = jnp.tanh(_dot(xw, w1_ref[...])).astype(BF16)
    z = -(w0_ref[...] + _dot(t, w2_ref[...]))
    softplus = jnp.maximum(z, 0.0) + jnp.log(1.0 + jnp.exp(-jnp.abs(z)))
    lw_ref[...] = -jnp.exp(-softplus - 0.5)

    xa = (h + dx * mu_ref[1]).astype(BF16)
    u = _dot(xa, a1_ref[...]).astype(BF16)
    al_ref[...] = jax.nn.sigmoid(a0_ref[...] + _dot(u, a2_ref[...]))

    xg = (h + dx * mu_ref[2]).astype(BF16)
    s = jax.nn.sigmoid(_dot(xg, g1_ref[...])).astype(BF16)
    gate_ref[...] = _dot(s, g2_ref[...])


def _lora_branches(h, hp, mu, w0, w1, w2, a0, a1, a2, g1, g2):
    n, d = h.shape
    row = pl.BlockSpec((ROW_TILE, d), lambda i: (i, 0))

    def whole(x):
        return pl.BlockSpec(x.shape, lambda i: (0,) * x.ndim)

    mu = mu.reshape(3, 1, d)
    w0 = w0.reshape(1, d)
    a0 = a0.reshape(1, d)
    consts = (mu, w0, w1, w2, a0, a1, a2, g1, g2)
    out = jax.ShapeDtypeStruct((n, d), F32)
    return pl.pallas_call(
        _lora_kernel,
        grid=(n // ROW_TILE,),
        in_specs=[row, row] + [whole(c) for c in consts],
        out_specs=[row, row, row],
        out_shape=[out, out, out],
        compiler_params=_params("parallel"),
        name="rwkv_lora",
    )(h, hp, *consts)


def _out_proj_kernel(o_ref, w_ref, g_ref, x_ref, y_ref):
    y = _dot(o_ref[...], w_ref[...])
    y_ref[...] = x_ref[...] + _rms(y) * g_ref[...]


def _out_proj(o, w, g, xres):
    n, d = xres.shape
    k = o.shape[1]
    return pl.pallas_call(
        _out_proj_kernel,
        grid=(n // ROW_TILE,),
        in_specs=[pl.BlockSpec((ROW_TILE, k), lambda i: (i, 0)),
                  pl.BlockSpec((k, d), lambda i: (0, 0)),
                  pl.BlockSpec((1, d), lambda i: (0, 0)),
                  pl.BlockSpec((ROW_TILE, d), lambda i: (i, 0))],
        out_specs=pl.BlockSpec((ROW_TILE, d), lambda i: (i, 0)),
        out_shape=jax.ShapeDtypeStruct((n, d), F32),
        compiler_params=_params("parallel"),
        name="out_proj",
    )(o, w, g.reshape(1, d), xres)


def _t5_bucket(rel):
    nb = N_BUCKETS // 2
    max_exact = nb // 2
    offset = jnp.where(rel > 0, nb, 0)
    n = jnp.abs(rel)
    nf = jnp.maximum(n, 1).astype(F32)
    large = max_exact + (jnp.log(nf / max_exact) / math.log(MAX_DISTANCE / max_exact)
                         * (nb - max_exact)).astype(jnp.int32)
    large = jnp.minimum(large, nb - 1)
    return offset + jnp.where(n < max_exact, n, large)


def _bias_kernel(tt_ref, b_ref, o_ref):
    onehot = (lax.broadcasted_iota(jnp.int32, (N_BUCKETS, b_ref.shape[1]), 0) == b_ref[...]).astype(F32)
    o_ref[...] = jnp.dot(tt_ref[...], onehot, preferred_element_type=F32,
                         precision=lax.Precision.HIGHEST)


def _rel_bias(table, n_q, n_k):
    rel = (jnp.arange(n_k, dtype=jnp.int32)[None, :] - WINDOW - jnp.arange(n_q, dtype=jnp.int32)[:, None])
    buckets = _t5_bucket(rel).reshape(1, n_q * n_k).astype(jnp.int32)
    n_heads = table.shape[1]
    out = pl.pallas_call(
        _bias_kernel,
        out_shape=jax.ShapeDtypeStruct((n_heads, n_q * n_k), F32),
        name="rel_bias",
    )(table.T, buckets)
    return out.reshape(n_heads, n_q, n_k)


def _attn_kernel(*refs, n_parts, group, first_valid_step):
    q_ref = refs[0]
    k_refs = refs[1:1 + n_parts]
    v_refs = refs[1 + n_parts:1 + 2 * n_parts]
    bias_ref, sink_ref, o_ref = refs[1 + 2 * n_parts:]
    n_q = q_ref.shape[0]
    step = pl.program_id(1)
    scale = HEAD_DIM ** -0.5

    part_rows = [r.shape[0] for r in k_refs]
    outs = []
    for hk in range(N_KV_HEADS):
        kv_cols = slice(hk * HEAD_DIM, (hk + 1) * HEAD_DIM)
        k = jnp.concatenate([r[:, kv_cols] for r in k_refs], axis=0).astype(BF16)
        v = jnp.concatenate([r[:, kv_cols] for r in v_refs], axis=0).astype(BF16)
        q = jnp.concatenate(
            [q_ref[:, (hk * group + g) * HEAD_DIM:(hk * group + g + 1) * HEAD_DIM] for g in range(group)],
            axis=0).astype(BF16)
        s = _dot_nt(q, k) * scale
        s = s + bias_ref[hk * group:(hk + 1) * group].reshape(group * n_q, -1)
        col = lax.broadcasted_iota(jnp.int32, s.shape, 1)
        start = 0
        for p, rows in enumerate(part_rows):
            if first_valid_step[p] > 0:
                hidden = (col >= start) & (col < start + rows) & (step < first_valid_step[p])
                s = jnp.where(hidden, NEG_INF, s)
            start += rows
        sink = sink_ref[hk * group * n_q:(hk + 1) * group * n_q]
        m = jnp.maximum(jnp.max(s, axis=-1, keepdims=True), sink)
        p_ = jnp.exp(s - m)
        denom = jnp.sum(p_, axis=-1, keepdims=True) + jnp.exp(sink - m)
        o = _dot((p_ / denom).astype(BF16), v)
        outs.extend(o[g * n_q:(g + 1) * n_q] for g in range(group))
    o_ref[...] = jnp.concatenate(outs, axis=1).astype(o_ref.dtype)


def _attention(q, k_parts, v_parts, part_index_maps, first_valid_step, bias, sinks, n_steps, q_rows):
    n_b, _, q_dim = q.shape
    n_heads = q_dim // HEAD_DIM
    group = n_heads // N_KV_HEADS
    n_parts = len(k_parts)
    kv_dim = N_KV_HEADS * HEAD_DIM
    part_rows = [rows for rows, _ in part_index_maps]
    n_k = sum(part_rows)
    assert bias.shape == (n_heads, q_rows, n_k)
    sink_col = jnp.repeat(sinks.astype(F32), q_rows).reshape(n_heads * q_rows, 1)
    kv_specs = [pl.BlockSpec((None, rows, kv_dim), imap) for rows, imap in part_index_maps]
    return pl.pallas_call(
        functools.partial(_attn_kernel, n_parts=n_parts, group=group,
                          first_valid_step=tuple(first_valid_step)),
        grid=(n_b, n_steps),
        in_specs=[pl.BlockSpec((None, q_rows, q_dim), lambda b, c: (b, c, 0))] + kv_specs + kv_specs + [
            pl.BlockSpec(bias.shape, lambda b, c: (0, 0, 0)),
            pl.BlockSpec(sink_col.shape, lambda b, c: (0, 0))],
        out_specs=pl.BlockSpec((None, q_rows, q_dim), lambda b, c: (b, c, 0)),
        out_shape=jax.ShapeDtypeStruct(q.shape, BF16),
        compiler_params=_params("parallel", "arbitrary"),
        name="swa_attention",
    )(q, *k_parts, *v_parts, bias, sink_col)


def _split3(x):
    hi = x.astype(BF16)
    r1 = x - hi.astype(F32)
    mid = r1.astype(BF16)
    lo = (r1 - mid.astype(F32)).astype(BF16)
    return hi, mid, lo


def _wkv_kernel(r_ref, k_ref, v_ref, lw_ref, al_ref, gate_ref, kk_ref, ka_ref, rk_ref, lnw_ref,
                lnb_ref, s0_ref, o_ref, s_out_ref, h_ref):
    L = r_ref.shape[0]
    L2 = 2 * L
    c_idx = pl.program_id(2)
    half = RWKV_HEAD

    lane = lax.broadcasted_iota(jnp.int32, (1, LANES), 1)
    head0 = lane < half
    row2 = lax.broadcasted_iota(jnp.int32, (L2, L2), 0)
    col2 = lax.broadcasted_iota(jnp.int32, (L2, L2), 1)
    strict_lower = row2 > col2
    lower = row2 >= col2
    eye2 = (row2 == col2).astype(F32)
    tril = (lax.broadcasted_iota(jnp.int32, (L, L), 0) >= lax.broadcasted_iota(jnp.int32, (L, L), 1)
            ).astype(BF16)
    krow = lax.broadcasted_iota(jnp.int32, (LANES, LANES), 0)
    kcol = lax.broadcasted_iota(jnp.int32, (LANES, LANES), 1)
    same_head = (krow < half) == (kcol < half)
    eye_k = (krow == kcol).astype(F32)

    def seg_sum(x):
        s0 = jnp.sum(jnp.where(head0, x, 0.0), axis=-1, keepdims=True)
        s1 = jnp.sum(jnp.where(head0, 0.0, x), axis=-1, keepdims=True)
        return jnp.where(head0, s0, s1)

    def stack2(x):
        return jnp.concatenate([jnp.where(head0, x, 0.0), jnp.where(head0, 0.0, x)], axis=0)

    def fold(m):
        return m[:L] + m[L:]

    @pl.when(c_idx == 0)
    def _():
        for p in range(WKV_PAIRS_PER_STEP):
            s0 = s0_ref[p]
            h_ref[p] = jnp.where(same_head, jnp.concatenate([s0, s0], axis=1), 0.0)

    pairs = range(WKV_PAIRS_PER_STEP)
    cols = [slice(p * LANES, (p + 1) * LANES) for p in pairs]

    r = [r_ref[:, cs] for cs in cols]
    v = [v_ref[:, cs] for cs in cols]
    lw = [lw_ref[:, cs] for cs in cols]
    c3 = [_dot(tril, jnp.concatenate(_split3(x), axis=1)) for x in lw]
    c = [x[:, :LANES] + x[:, LANES:2 * LANES] + x[:, 2 * LANES:] for x in c3]

    k, a, b = [], [], []
    for p in pairs:
        k_raw = k_ref[:, cols[p]]
        al = al_ref[:, cols[p]]
        kk = k_raw * kk_ref[:, cols[p]]
        kk = kk / jnp.maximum(jnp.sqrt(seg_sum(kk * kk)), 1e-12)
        k.append(k_raw * (1.0 + (al - 1.0) * ka_ref[:, cols[p]]))
        a.append(-kk)
        b.append(kk * al)

    at2, v2, rt, bc2, kc2, d_end, gram = [], [], [], [], [], [], []
    for p in pairs:
        c_last = c[p][L - 1:L]
        e_neg = jnp.exp(-c[p])
        e_end = jnp.exp(c_last - c[p])
        at2.append(stack2(a[p] * jnp.exp(c[p] - lw[p])))
        rt.append(r[p] * jnp.exp(c[p]))
        v2.append(stack2(v[p]))
        bc2.append(stack2(b[p] * e_end))
        kc2.append(stack2(k[p] * e_end))
        d_end.append(jnp.exp(c_last))
        gram.append(_dot_nt(
            jnp.concatenate([at2[p], stack2(rt[p])], axis=0).astype(BF16),
            jnp.concatenate([stack2(b[p] * e_neg), stack2(k[p] * e_neg)], axis=0).astype(BF16)))

    n_mat = [jnp.where(strict_lower, g[:L2, :L2], 0.0) for g in gram]
    akv2 = [_dot(jnp.where(strict_lower, gram[p][:L2, L2:], 0.0).astype(BF16), v2[p].astype(BF16))
            for p in pairs]
    lhs_y = [jnp.concatenate([fold(jnp.where(lower, g[L2:, :L2], 0.0)),
                              fold(jnp.where(lower, g[L2:, L2:], 0.0))], axis=1) for g in gram]

    size = 1
    t_inv = [eye2 for _ in pairs]
    while size < L:
        quadrant = ((row2 // (2 * size)) == (col2 // (2 * size))) & \
                   ((row2 // size) % 2 == 1) & ((col2 // size) % 2 == 0)
        n_c = [jnp.where(quadrant, n, 0.0) for n in n_mat]
        if size == 1:
            t_inv = [t + n for t, n in zip(t_inv, n_c)]
        else:
            t_bf = [t.astype(BF16) for t in t_inv]
            tn = [_dot(t, n.astype(BF16)).astype(BF16) for t, n in zip(t_bf, n_c)]
            t_inv = [t + _dot(x, tb) for t, x, tb in zip(t_inv, tn, t_bf)]
        size *= 2

    u2 = [_dot(t_inv[p].astype(BF16), jnp.concatenate([at2[p], akv2[p]], axis=1).astype(BF16))
          for p in pairs]
    yz = []
    for p in pairs:
        z = jnp.concatenate(
            [u2[p], jnp.concatenate([jnp.zeros((L2, LANES), F32), v2[p]], axis=1)], axis=0).astype(BF16)
        lhs_h = jnp.concatenate([bc2[p], kc2[p]], axis=0).T
        yz.append(_dot(jnp.concatenate([lhs_y[p], lhs_h], axis=0).astype(BF16), z))

    hh = []
    for p in pairs:
        q_hat = rt[p] + yz[p][:L, :LANES]
        m_mat = eye_k * d_end[p] + yz[p][L:, :LANES]
        hh.append(_dot(jnp.concatenate([m_mat, q_hat], axis=0).astype(BF16), h_ref[p].astype(BF16)))

    for p in pairs:
        h_ref[p] = jnp.where(same_head, hh[p][:LANES] + yz[p][L:, LANES:], 0.0)
        y = hh[p][LANES:] + yz[p][:L, LANES:]
        mean = seg_sum(y) * (1.0 / half)
        yc = y - mean
        var = seg_sum(yc * yc) * (1.0 / half)
        yn = yc * lax.rsqrt(var + GN_EPS) * lnw_ref[:, cols[p]] + lnb_ref[:, cols[p]]
        bonus = seg_sum(r[p] * k[p] * rk_ref[:, cols[p]]) * v[p]
        o_ref[:, cols[p]] = ((yn + bonus) * gate_ref[:, cols[p]]).astype(o_ref.dtype)

    @pl.when(c_idx == pl.num_programs(2) - 1)
    def _():
        for p in range(WKV_PAIRS_PER_STEP):
            h_last = h_ref[p]
            s_out_ref[p] = h_last[:, :half] + h_last[:, half:]


def _wkv(r, k, v, lw, al, gate, k_k, k_a, r_k, ln_w, ln_b, state):
    n_b, t, d = r.shape
    n_pairs = d // LANES
    step_lanes = WKV_PAIRS_PER_STEP * LANES
    n_groups = n_pairs // WKV_PAIRS_PER_STEP
    s_in = jnp.swapaxes(state, -1, -2).reshape(n_b, n_pairs, LANES, RWKV_HEAD)
    seq = pl.BlockSpec((None, WKV_CHUNK, step_lanes), lambda b, g, c: (b, c, g))
    vec = pl.BlockSpec((1, step_lanes), lambda b, g, c: (0, g))
    st = pl.BlockSpec((None, WKV_PAIRS_PER_STEP, LANES, RWKV_HEAD), lambda b, g, c: (b, g, 0, 0))
    o, s_out = pl.pallas_call(
        _wkv_kernel,
        grid=(n_b, n_groups, t // WKV_CHUNK),
        in_specs=[seq] * 6 + [vec] * 5 + [st],
        out_specs=[seq, st],
        out_shape=[jax.ShapeDtypeStruct((n_b, t, d), BF16),
                   jax.ShapeDtypeStruct(s_in.shape, F32)],
        scratch_shapes=[pltpu.VMEM((WKV_PAIRS_PER_STEP, LANES, LANES), F32)],
        compiler_params=_params("parallel", "parallel", "arbitrary"),
        name="rwkv7_wkv",
    )(r, k, v, lw, al, gate, *[x.reshape(1, d) for x in (k_k, k_a, r_k, ln_w, ln_b)], s_in)
    s_out = jnp.swapaxes(s_out.reshape(n_b, d // RWKV_HEAD, RWKV_HEAD, RWKV_HEAD), -1, -2)
    return o, s_out


def _pad_lora(w_in, w_out):
    rank = w_in.shape[1]
    pad = (-rank) % LORA_PAD
    return (jnp.pad(w_in, ((0, 0), (0, pad))).astype(BF16),
            jnp.pad(w_out, ((0, pad), (0, 0))).astype(BF16))


def kernel(x_prompt, x_sample, cache_k, cache_v, state_shift, state_wkv, norm_g, ffn_w_gate, ffn_w_up, ffn_w_down, rel_table, att_w_qkv, att_b_qkv, att_w_o, att_sinks, rwkv_mu, rwkv_w_r, rwkv_w_k, rwkv_w_v, rwkv_w_o, rwkv_w0, rwkv_w1, rwkv_w2, rwkv_a0, rwkv_a1, rwkv_a2, rwkv_g1, rwkv_g2, rwkv_k_k, rwkv_k_a, rwkv_r_k, rwkv_ln_w, rwkv_ln_b):
    n_bp, t_p, d = x_prompt.shape
    n_bs, t_s, _ = x_sample.shape
    n_p = n_bp * t_p
    depth = norm_g.shape[0]
    q_dim = att_w_o.shape[1]
    kv_dim = N_KV_HEADS * HEAD_DIM

    x = jnp.concatenate([x_prompt.reshape(n_p, d), x_sample.reshape(n_bs * t_s, d)], axis=0)

    wg = ffn_w_gate.astype(BF16)
    wu = ffn_w_up.astype(BF16)
    wd = ffn_w_down.astype(BF16)

    new_k, new_v, new_shift, new_wkv = [], [], [], []
    for i in range(depth):
        g = norm_g[i]
        j = i // 2
        x = _ffn_block(x, g[0], g[1], wg[i, 0], wu[i, 0], wd[i, 0])
        if i % 2 == 0:
            qkv = _qkv_proj(x, g[2], att_w_qkv[j].astype(BF16), att_b_qkv[j], col_tile=512)
            qkv_p = qkv[:n_p].reshape(n_bp, t_p, -1)
            qkv_s = qkv[n_p:].reshape(n_bs, t_s, -1)
            q_p, k_p, v_p = qkv_p[..., :q_dim], qkv_p[..., q_dim:q_dim + kv_dim], qkv_p[..., q_dim + kv_dim:]
            q_s, k_s, v_s = qkv_s[..., :q_dim], qkv_s[..., q_dim:q_dim + kv_dim], qkv_s[..., q_dim + kv_dim:]

            n_back = WINDOW // CHUNK
            maps = [(CHUNK, (lambda b, c, back=back: (b, jnp.maximum(c - back, 0), 0)))
                    for back in range(n_back, -1, -1)]
            bias_p = _rel_bias(rel_table, CHUNK, (n_back + 1) * CHUNK)
            o_p = _attention(q_p, [k_p] * (n_back + 1), [v_p] * (n_back + 1), maps,
                             [back for back in range(n_back, -1, -1)], bias_p, att_sinks[j],
                             n_steps=t_p // CHUNK, q_rows=CHUNK)
            ck = cache_k[j].reshape(n_bs, WINDOW, kv_dim)
            cv = cache_v[j].reshape(n_bs, WINDOW, kv_dim)
            bias_s = _rel_bias(rel_table, t_s, WINDOW + t_s)
            maps_s = [(WINDOW, lambda b, c: (b, 0, 0)), (t_s, lambda b, c: (b, 0, 0))]
            o_s = _attention(q_s, [ck, k_s], [cv, v_s], maps_s, [0, 0], bias_s, att_sinks[j],
                             n_steps=1, q_rows=t_s)
            o = jnp.concatenate([o_p.reshape(n_p, q_dim), o_s.reshape(n_bs * t_s, q_dim)], axis=0)
            x = _out_proj(o, att_w_o[j].astype(BF16), g[3], x)
            new_k.append((k_p[:, t_p - WINDOW:].reshape(n_bp, WINDOW, N_KV_HEADS, HEAD_DIM),
                          k_s.reshape(n_bs, t_s, N_KV_HEADS, HEAD_DIM)))
            new_v.append((v_p[:, t_p - WINDOW:].reshape(n_bp, WINDOW, N_KV_HEADS, HEAD_DIM),
                          v_s.reshape(n_bs, t_s, N_KV_HEADS, HEAD_DIM)))
        else:
            h = _norm(x, g[2])
            h_p = h[:n_p].reshape(n_bp, t_p, d)
            h_s = h[n_p:].reshape(n_bs, t_s, d)
            hp = jnp.concatenate([
                jnp.concatenate([jnp.zeros((n_bp, 1, d), F32), h_p[:, :-1]], axis=1).reshape(n_p, d),
                jnp.concatenate([state_shift[j], h_s[:, :-1]], axis=1).reshape(n_bs * t_s, d)], axis=0)
            mu = rwkv_mu[j]
            rkv = _rkv_proj(h, hp, jnp.stack([mu[0], mu[2], mu[3]]),
                            jnp.stack([rwkv_w_r[j], rwkv_w_k[j], rwkv_w_v[j]]).astype(BF16), col_tile=512)
            w1, w2 = _pad_lora(rwkv_w1[j], rwkv_w2[j])
            a1, a2 = _pad_lora(rwkv_a1[j], rwkv_a2[j])
            g1, g2 = _pad_lora(rwkv_g1[j], rwkv_g2[j])
            lw, al, gate = _lora_branches(h, hp, jnp.stack([mu[1], mu[4], mu[5]]), rwkv_w0[j], w1, w2,
                                          rwkv_a0[j], a1, a2, g1, g2)
            vecs = (rwkv_k_k[j], rwkv_k_a[j], rwkv_r_k[j].reshape(d), rwkv_ln_w[j], rwkv_ln_b[j])

            def split(a):
                return a[:n_p].reshape(n_bp, t_p, d), a[n_p:].reshape(n_bs, t_s, d)

            seqs = [split(a) for a in (rkv[0], rkv[1], rkv[2], lw, al, gate)]
            zero_state = jnp.zeros((n_bp, d // RWKV_HEAD, RWKV_HEAD, RWKV_HEAD), F32)
            o_p, wkv_p = _wkv(*[s[0] for s in seqs], *vecs, zero_state)
            pad = (-t_s) % WKV_CHUNK
            o_s, wkv_s = _wkv(*[jnp.pad(s[1], ((0, 0), (0, pad), (0, 0))) for s in seqs], *vecs,
                              state_wkv[j])
            o = jnp.concatenate([o_p.reshape(n_p, d), o_s[:, :t_s].reshape(n_bs * t_s, d)], axis=0)
            x = _out_proj(o, rwkv_w_o[j].astype(BF16), g[3], x)
            new_shift.append((h_p[:, t_p - 1:], h_s[:, t_s - 1:]))
            new_wkv.append((wkv_p, wkv_s))
        x = _ffn_block(x, g[4], g[5], wg[i, 1], wu[i, 1], wd[i, 1])

    def both(pairs):
        return jnp.stack([p[0] for p in pairs]), jnp.stack([p[1] for p in pairs])

    k_prompt, k_sample = both(new_k)
    v_prompt, v_sample = both(new_v)
    shift_prompt, shift_sample = both(new_shift)
    wkv_prompt, wkv_sample = both(new_wkv)
    return (x[:n_p].reshape(n_bp, t_p, d), x[n_p:].reshape(n_bs, t_s, d),
            k_prompt, v_prompt, k_sample, v_sample,
            shift_prompt, wkv_prompt, shift_sample, wkv_sample)
```

```python
import functools
import math

import jax
import jax.numpy as jnp
from jax import lax
from jax.experimental import pallas as pl
from jax.experimental.pallas import tpu as pltpu

F32 = jnp.float32
BF16 = jnp.bfloat16

HEAD_DIM = 64
N_KV_HEADS = 4
CHUNK = 64
WINDOW = 128
N_BUCKETS = 32
MAX_DISTANCE = 128
RWKV_HEAD = 64
RMS_EPS = 1e-6
GN_EPS = RWKV_HEAD * 1e-5
NEG_INF = -1e30
LORA_PAD = 128

LANES = 128
VMEM_LIMIT_BYTES = 56 * 1024 * 1024
ROW_TILE = 512
FF_TILE = 512
WKV_CHUNK = 64


def _params(*semantics):
    return pltpu.CompilerParams(dimension_semantics=semantics,
                                vmem_limit_bytes=VMEM_LIMIT_BYTES)


def _rms(x):
    return x * lax.rsqrt(jnp.mean(jnp.square(x), axis=-1, keepdims=True) + RMS_EPS)


def _dot(a, b):
    return jnp.dot(a, b, preferred_element_type=F32)


def _dot_nt(a, b):
    return lax.dot_general(a, b, (((1,), (1,)), ((), ())), preferred_element_type=F32)


def _two_part_specs(d, prompt_tiles):
    return [pl.BlockSpec((ROW_TILE, d), lambda i, *_: (jnp.minimum(i, prompt_tiles - 1), 0)),
            pl.BlockSpec((ROW_TILE, d), lambda i, *_: (jnp.maximum(i - prompt_tiles, 0), 0))]


def _ffn_kernel(x_ref, g0_ref, g1_ref, wg_ref, wu_ref, wd_ref, o_ref, h_ref):
    f = pl.program_id(1)

    @pl.when(f == 0)
    def _():
        h_ref[...] = (_rms(x_ref[...]) * g0_ref[...]).astype(BF16)
        o_ref[...] = jnp.zeros_like(o_ref)

    h = h_ref[...]
    a = _dot(h, wg_ref[...])
    b = _dot(h, wu_ref[...])
    act = (a * jax.nn.sigmoid(a) * b).astype(BF16)
    o_ref[...] += _dot(act, wd_ref[...])

    @pl.when(f == pl.num_programs(1) - 1)
    def _():
        o_ref[...] = x_ref[...] + 0.5 * (_rms(o_ref[...]) * g1_ref[...])


def _ffn_block(x, g0, g1, wg, wu, wd):
    n, d = x.shape
    d_ff = wg.shape[1]
    return pl.pallas_call(
        _ffn_kernel,
        grid=(n // ROW_TILE, d_ff // FF_TILE),
        in_specs=[
            pl.BlockSpec((ROW_TILE, d), lambda i, f: (i, 0)),
            pl.BlockSpec((1, d), lambda i, f: (0, 0)),
            pl.BlockSpec((1, d), lambda i, f: (0, 0)),
            pl.BlockSpec((d, FF_TILE), lambda i, f: (0, f)),
            pl.BlockSpec((d, FF_TILE), lambda i, f: (0, f)),
            pl.BlockSpec((FF_TILE, d), lambda i, f: (f, 0)),
        ],
        out_specs=pl.BlockSpec((ROW_TILE, d), lambda i, f: (i, 0)),
        out_shape=jax.ShapeDtypeStruct((n, d), F32),
        scratch_shapes=[pltpu.VMEM((ROW_TILE, d), BF16)],
        compiler_params=_params("parallel", "arbitrary"),
        name="ffn_block",
    )(x, g0.reshape(1, d), g1.reshape(1, d), wg, wu, wd)


def _norm_shift_kernel(x_ref, g_ref, first_ref, h_ref, hp_ref, carry_ref, *, prompt_tiles, prompt_len,
                       sample_len):
    i = pl.program_id(0)

    @pl.when(i == 0)
    def _():
        carry_ref[...] = jnp.zeros_like(carry_ref)

    h = _rms(x_ref[...]) * g_ref[...]
    h_ref[...] = h
    rows = lax.broadcasted_iota(jnp.int32, (ROW_TILE, 1), 0)
    prev = jnp.where(rows == 0, carry_ref[...], pltpu.roll(h, 1, axis=0))

    @pl.when(i < prompt_tiles)
    def _():
        hp_ref[...] = jnp.where((rows + i * ROW_TILE) % prompt_len == 0, 0.0, prev)

    @pl.when(i >= prompt_tiles)
    def _():
        hp_ref[...] = jnp.where(rows % sample_len == 0, first_ref[...], prev)

    carry_ref[...] = h[ROW_TILE - 1:ROW_TILE]


def _norm_shift(x, g, first_rows, n_prompt_rows, prompt_len, sample_len):
    n, d = x.shape
    prompt_tiles = n_prompt_rows // ROW_TILE
    row = pl.BlockSpec((ROW_TILE, d), lambda i: (i, 0))
    out = jax.ShapeDtypeStruct((n, d), F32)
    return pl.pallas_call(
        functools.partial(_norm_shift_kernel, prompt_tiles=prompt_tiles, prompt_len=prompt_len,
                          sample_len=sample_len),
        grid=(n // ROW_TILE,),
        in_specs=[row, pl.BlockSpec((1, d), lambda i: (0, 0)),
                  pl.BlockSpec((ROW_TILE, d), lambda i: (jnp.maximum(i - prompt_tiles, 0), 0))],
        out_specs=[row, row],
        out_shape=[out, out],
        scratch_shapes=[pltpu.VMEM((1, d), F32)],
        compiler_params=_params("arbitrary"),
        name="norm_shift",
    )(x, g.reshape(1, d), first_rows)


def _qkv_kernel(x_ref, g_ref, w_ref, b_ref, q_ref, k_ref, v_ref):
    h = (_rms(x_ref[...]) * g_ref[...]).astype(BF16)
    y = _dot(h, w_ref[...]) + b_ref[...]
    q_dim = q_ref.shape[1]
    kv_dim = k_ref.shape[1]
    q_ref[...] = y[:, :q_dim].astype(q_ref.dtype)
    k_ref[...] = y[:, q_dim:q_dim + kv_dim]
    v_ref[...] = y[:, q_dim + kv_dim:]


def _qkv_proj(x, g, w, b, q_dim, kv_dim):
    n, d = x.shape
    m = w.shape[1]

    def rows(width):
        return pl.BlockSpec((ROW_TILE, width), lambda i: (i, 0))

    return pl.pallas_call(
        _qkv_kernel,
        grid=(n // ROW_TILE,),
        in_specs=[rows(d), pl.BlockSpec((1, d), lambda i: (0, 0)),
                  pl.BlockSpec((d, m), lambda i: (0, 0)), pl.BlockSpec((1, m), lambda i: (0, 0))],
        out_specs=[rows(q_dim), rows(kv_dim), rows(kv_dim)],
        out_shape=[jax.ShapeDtypeStruct((n, q_dim), BF16), jax.ShapeDtypeStruct((n, kv_dim), F32),
                   jax.ShapeDtypeStruct((n, kv_dim), F32)],
        compiler_params=_params("parallel"),
        name="qkv_proj",
    )(x, g.reshape(1, d), w, b.reshape(1, m))


def _rkv_kernel(h_ref, hp_ref, mu_ref, w_ref, o_ref):
    h = h_ref[...]
    x = (h + (hp_ref[...] - h) * mu_ref[...]).astype(BF16)
    o_ref[...] = _dot(x, w_ref[...])


def _rkv_proj(h, hp, mu, w):
    n, d = h.shape
    k = w.shape[0]
    m = w.shape[2]
    return pl.pallas_call(
        _rkv_kernel,
        grid=(n // ROW_TILE, k),
        in_specs=[pl.BlockSpec((ROW_TILE, d), lambda i, s: (i, 0)),
                  pl.BlockSpec((ROW_TILE, d), lambda i, s: (i, 0)),
                  pl.BlockSpec((None, 1, d), lambda i, s: (s, 0, 0)),
                  pl.BlockSpec((None, d, m), lambda i, s: (s, 0, 0))],
        out_specs=pl.BlockSpec((None, ROW_TILE, m), lambda i, s: (s, i, 0)),
        out_shape=jax.ShapeDtypeStruct((k, n, m), F32),
        compiler_params=_params("parallel", "arbitrary"),
        name="rkv_proj",
    )(h, hp, mu.reshape(k, 1, d), w)


def _lora_kernel(h_ref, hp_ref, mu_ref, w0_ref, w1_ref, w2_ref, a0_ref, a1_ref, a2_ref,
                 g1_ref, g2_ref, lw_ref, al_ref, gate_ref):
    h = h_ref[...]
    dx = hp_ref[...] - h

    xw = (h + dx * mu_ref[0]).astype(BF16)
    t = jnp.tanh(_dot(xw, w1_ref[...])).astype(BF16)
    z = -(w0_ref[...] + _dot(t, w2_ref[...]))
    softplus = jnp.maximum(z, 0.0) + jnp.log(1.0 + jnp.exp(-jnp.abs(z)))
    lw_ref[...] = -jnp.exp(-softplus - 0.5)

    xa = (h + dx * mu_ref[1]).astype(BF16)
    u = _dot(xa, a1_ref[...]).astype(BF16)
    al_ref[...] = jax.nn.sigmoid(a0_ref[...] + _dot(u, a2_ref[...]))

    xg = (h + dx * mu_ref[2]).astype(BF16)
    s = jax.nn.sigmoid(_dot(xg, g1_ref[...])).astype(BF16)
    gate_ref[...] = _dot(s, g2_ref[...])


def _lora_branches(h, hp, mu, w0, w1, w2, a0, a1, a2, g1, g2):
    n, d = h.shape
    row = pl.BlockSpec((ROW_TILE, d), lambda i: (i, 0))

    def whole(x):
        return pl.BlockSpec(x.shape, lambda i: (0,) * x.ndim)

    mu = mu.reshape(3, 1, d)
    w0 = w0.reshape(1, d)
    a0 = a0.reshape(1, d)
    consts = (mu, w0, w1, w2, a0, a1, a2, g1, g2)
    out = jax.ShapeDtypeStruct((n, d), F32)
    return pl.pallas_call(
        _lora_kernel,
        grid=(n // ROW_TILE,),
        in_specs=[row, row] + [whole(c) for c in consts],
        out_specs=[row, row, row],
        out_shape=[out, out, out],
        compiler_params=_params("parallel"),
        name="rwkv_lora",
    )(h, hp, *consts)


def _out_proj_kernel(op_ref, os_ref, w_ref, g_ref, x_ref, y_ref, *, prompt_tiles):
    o = jnp.where(pl.program_id(0) < prompt_tiles, op_ref[...], os_ref[...])
    y = _dot(o, w_ref[...])
    y_ref[...] = x_ref[...] + _rms(y) * g_ref[...]


def _out_proj(o_prompt, o_sample, w, g, xres):
    n, d = xres.shape
    k = o_prompt.shape[1]
    prompt_tiles = o_prompt.shape[0] // ROW_TILE
    return pl.pallas_call(
        functools.partial(_out_proj_kernel, prompt_tiles=prompt_tiles),
        grid=(n // ROW_TILE,),
        in_specs=_two_part_specs(k, prompt_tiles) + [
            pl.BlockSpec((k, d), lambda i: (0, 0)),
            pl.BlockSpec((1, d), lambda i: (0, 0)),
            pl.BlockSpec((ROW_TILE, d), lambda i: (i, 0))],
        out_specs=pl.BlockSpec((ROW_TILE, d), lambda i: (i, 0)),
        out_shape=jax.ShapeDtypeStruct((n, d), F32),
        compiler_params=_params("arbitrary"),
        name="out_proj",
    )(o_prompt, o_sample, w, g.reshape(1, d), xres)


def _t5_bucket(rel):
    nb = N_BUCKETS // 2
    max_exact = nb // 2
    offset = jnp.where(rel > 0, nb, 0)
    n = jnp.abs(rel)
    nf = jnp.maximum(n, 1).astype(F32)
    large = max_exact + (jnp.log(nf / max_exact) / math.log(MAX_DISTANCE / max_exact)
                         * (nb - max_exact)).astype(jnp.int32)
    large = jnp.minimum(large, nb - 1)
    return offset + jnp.where(n < max_exact, n, large)


def _bias_kernel(tt_ref, b_ref, o_ref):
    onehot = (lax.broadcasted_iota(jnp.int32, (N_BUCKETS, b_ref.shape[1]), 0) == b_ref[...]).astype(F32)
    o_ref[...] = jnp.dot(tt_ref[...], onehot, preferred_element_type=F32,
                         precision=lax.Precision.HIGHEST)


def _rel_bias(table, n_q, n_k):
    rel = (jnp.arange(n_k, dtype=jnp.int32)[None, :] - WINDOW - jnp.arange(n_q, dtype=jnp.int32)[:, None])
    buckets = _t5_bucket(rel).reshape(1, n_q * n_k).astype(jnp.int32)
    n_heads = table.shape[1]
    out = pl.pallas_call(
        _bias_kernel,
        out_shape=jax.ShapeDtypeStruct((n_heads, n_q * n_k), F32),
        name="rel_bias",
    )(table.T, buckets)
    return out.reshape(n_heads, n_q, n_k)


def _attn_kernel(*refs, n_parts, group, first_valid_step):
    q_ref = refs[0]
    k_refs = refs[1:1 + n_parts]
    v_refs = refs[1 + n_parts:1 + 2 * n_parts]
    bias_ref, sink_ref, o_ref = refs[1 + 2 * n_parts:]
    n_q = q_ref.shape[0]
    step = pl.program_id(1)
    scale = HEAD_DIM ** -0.5
    part_rows = [r.shape[0] for r in k_refs]
    n_k = sum(part_rows)
    heads = range(N_KV_HEADS)

    col = lax.broadcasted_iota(jnp.int32, (1, n_k), 1)
    hidden = None
    start = 0
    for p, rows in enumerate(part_rows):
        if first_valid_step[p] > 0:
            h_p = (col >= start) & (col < start + rows) & (step < first_valid_step[p])
            hidden = h_p if hidden is None else hidden | h_p
        start += rows

    scores = []
    for hk in heads:
        kv_cols = slice(hk * HEAD_DIM, (hk + 1) * HEAD_DIM)
        k = jnp.concatenate([r[:, kv_cols] for r in k_refs], axis=0).astype(BF16)
        q = jnp.concatenate(
            [q_ref[:, (hk * group + g) * HEAD_DIM:(hk * group + g + 1) * HEAD_DIM] for g in range(group)],
            axis=0)
        s = _dot_nt(q, k) * scale + bias_ref[hk * group:(hk + 1) * group].reshape(group * n_q, n_k)
        scores.append(s if hidden is None else jnp.where(hidden, NEG_INF, s))

    probs = []
    for hk in heads:
        sink = sink_ref[hk * group * n_q:(hk + 1) * group * n_q]
        m = jnp.maximum(jnp.max(scores[hk], axis=-1, keepdims=True), sink)
        e = jnp.exp(scores[hk] - m)
        denom = jnp.sum(e, axis=-1, keepdims=True) + jnp.exp(sink - m)
        probs.append((e / denom).astype(BF16))

    outs = []
    for hk in heads:
        kv_cols = slice(hk * HEAD_DIM, (hk + 1) * HEAD_DIM)
        v = jnp.concatenate([r[:, kv_cols] for r in v_refs], axis=0).astype(BF16)
        o = _dot(probs[hk], v)
        outs.extend(o[g * n_q:(g + 1) * n_q] for g in range(group))
    o_ref[...] = jnp.concatenate(outs, axis=1).astype(o_ref.dtype)


def _attention(q, parts, bias, sinks, *, grid, q_rows, q_map, n_out_rows):
    q_dim = q.shape[1]
    n_heads = q_dim // HEAD_DIM
    group = n_heads // N_KV_HEADS
    n_k = sum(p[2] for p in parts)
    assert bias.shape == (n_heads, q_rows, n_k)
    sink_col = jnp.repeat(sinks.astype(F32), q_rows).reshape(n_heads * q_rows, 1)

    def kv_spec(arr, rows, imap):
        if arr.ndim == 3:
            return pl.BlockSpec((None, rows, arr.shape[2]), imap)
        return pl.BlockSpec((rows, arr.shape[1]), imap)

    k_specs = [kv_spec(p[0], p[2], p[3]) for p in parts]
    v_specs = [kv_spec(p[1], p[2], p[3]) for p in parts]
    n_steps = grid[1]
    return pl.pallas_call(
        functools.partial(_attn_kernel, n_parts=len(parts), group=group,
                          first_valid_step=tuple(p[4] for p in parts)),
        grid=grid,
        in_specs=[pl.BlockSpec((q_rows, q_dim), q_map)] + k_specs + v_specs + [
            pl.BlockSpec(bias.shape, lambda b, c: (0, 0, 0)),
            pl.BlockSpec(sink_col.shape, lambda b, c: (0, 0))],
        out_specs=pl.BlockSpec((q_rows, q_dim), lambda b, c: (b * n_steps + c, 0)),
        out_shape=jax.ShapeDtypeStruct((n_out_rows, q_dim), BF16),
        compiler_params=_params("parallel", "arbitrary"),
        name="swa_attention",
    )(q, *[p[0] for p in parts], *[p[1] for p in parts], bias, sink_col)


def _split3(x):
    hi = x.astype(BF16)
    r1 = x - hi.astype(F32)
    mid = r1.astype(BF16)
    lo = (r1 - mid.astype(F32)).astype(BF16)
    return hi, mid, lo


def _wkv_kernel(r_ref, k_ref, v_ref, lw_ref, al_ref, gate_ref, kk_ref, ka_ref, rk_ref, lnw_ref,
                lnb_ref, s0_ref, o_ref, s_out_ref, h_ref):
    L = WKV_CHUNK
    L2 = 2 * L
    valid = r_ref.shape[0]
    n_pairs = r_ref.shape[1] // LANES
    c_idx = pl.program_id(1)
    half = RWKV_HEAD

    lane = lax.broadcasted_iota(jnp.int32, (1, LANES), 1)
    head0 = lane < half
    row2 = lax.broadcasted_iota(jnp.int32, (L2, L2), 0)
    col2 = lax.broadcasted_iota(jnp.int32, (L2, L2), 1)
    strict_lower = row2 > col2
    lower = row2 >= col2
    eye2 = (row2 == col2).astype(F32)
    tril = (lax.broadcasted_iota(jnp.int32, (L, L), 0) >= lax.broadcasted_iota(jnp.int32, (L, L), 1)
            ).astype(BF16)
    krow = lax.broadcasted_iota(jnp.int32, (LANES, LANES), 0)
    kcol = lax.broadcasted_iota(jnp.int32, (LANES, LANES), 1)
    same_head = (krow < half) == (kcol < half)
    eye_k = (krow == kcol).astype(F32)

    def load(ref, cs):
        x = ref[:, cs]
        if valid < L:
            x = jnp.concatenate([x, jnp.zeros((L - valid, LANES), x.dtype)], axis=0)
        return x

    def seg_sum(x):
        s0 = jnp.sum(jnp.where(head0, x, 0.0), axis=-1, keepdims=True)
        s1 = jnp.sum(jnp.where(head0, 0.0, x), axis=-1, keepdims=True)
        return jnp.where(head0, s0, s1)

    def stack2(x):
        return jnp.concatenate([jnp.where(head0, x, 0.0), jnp.where(head0, 0.0, x)], axis=0)

    def fold(m):
        return m[:L] + m[L:]

    @pl.when(c_idx == 0)
    def _():
        for p in range(n_pairs):
            s0 = s0_ref[p]
            h_ref[p] = jnp.where(same_head, jnp.concatenate([s0, s0], axis=1), 0.0)

    pairs = range(n_pairs)
    cols = [slice(p * LANES, (p + 1) * LANES) for p in pairs]

    r = [load(r_ref, cs) for cs in cols]
    v = [load(v_ref, cs) for cs in cols]
    lw = [load(lw_ref, cs) for cs in cols]
    c3 = [_dot(tril, jnp.concatenate(_split3(x), axis=1)) for x in lw]
    c = [x[:, :LANES] + x[:, LANES:2 * LANES] + x[:, 2 * LANES:] for x in c3]

    k, a, b = [], [], []
    for p in pairs:
        k_raw = load(k_ref, cols[p])
        al = load(al_ref, cols[p])
        kk = k_raw * kk_ref[:, cols[p]]
        kk = kk / jnp.maximum(jnp.sqrt(seg_sum(kk * kk)), 1e-12)
        k.append(k_raw * (1.0 + (al - 1.0) * ka_ref[:, cols[p]]))
        a.append(-kk)
        b.append(kk * al)

    at2, v2, rt, bc2, kc2, d_end, gram = [], [], [], [], [], [], []
    for p in pairs:
        c_last = c[p][L - 1:L]
        e_neg = jnp.exp(-c[p])
        e_end = jnp.exp(c_last - c[p])
        at2.append(stack2(a[p] * jnp.exp(c[p] - lw[p])))
        rt.append(r[p] * jnp.exp(c[p]))
        v2.append(stack2(v[p]))
        bc2.append(stack2(b[p] * e_end))
        kc2.append(stack2(k[p] * e_end))
        d_end.append(jnp.exp(c_last))
        gram.append(_dot_nt(
            jnp.concatenate([at2[p], stack2(rt[p])], axis=0).astype(BF16),
            jnp.concatenate([stack2(b[p] * e_neg), stack2(k[p] * e_neg)], axis=0).astype(BF16)))

    n_mat = [jnp.where(strict_lower, g[:L2, :L2], 0.0) for g in gram]
    akv2 = [_dot(jnp.where(strict_lower, gram[p][:L2, L2:], 0.0).astype(BF16), v2[p].astype(BF16))
            for p in pairs]
    lhs_y = [jnp.concatenate([fold(jnp.where(lower, g[L2:, :L2], 0.0)),
                              fold(jnp.where(lower, g[L2:, L2:], 0.0))], axis=1) for g in gram]

    size = 1
    t_inv = [eye2 for _ in pairs]
    while size < L:
        quadrant = ((row2 // (2 * size)) == (col2 // (2 * size))) & \
                   ((row2 // size) % 2 == 1) & ((col2 // size) % 2 == 0)
        n_c = [jnp.where(quadrant, n, 0.0) for n in n_mat]
        if size == 1:
            t_inv = [t + n for t, n in zip(t_inv, n_c)]
        else:
            t_bf = [t.astype(BF16) for t in t_inv]
            tn = [_dot(t, n.astype(BF16)).astype(BF16) for t, n in zip(t_bf, n_c)]
            t_inv = [t + _dot(x, tb) for t, x, tb in zip(t_inv, tn, t_bf)]
        size *= 2

    u2 = [_dot(t_inv[p].astype(BF16), jnp.concatenate([at2[p], akv2[p]], axis=1).astype(BF16))
          for p in pairs]
    yz = []
    for p in pairs:
        z = jnp.concatenate(
            [u2[p], jnp.concatenate([jnp.zeros((L2, LANES), F32), v2[p]], axis=1)], axis=0).astype(BF16)
        lhs_h = jnp.concatenate([bc2[p], kc2[p]], axis=0).T
        yz.append(_dot(jnp.concatenate([lhs_y[p], lhs_h], axis=0).astype(BF16), z))

    hh = []
    for p in pairs:
        q_hat = rt[p] + yz[p][:L, :LANES]
        m_mat = eye_k * d_end[p] + yz[p][L:, :LANES]
        hh.append(_dot(jnp.concatenate([m_mat, q_hat], axis=0).astype(BF16), h_ref[p].astype(BF16)))

    for p in pairs:
        h_ref[p] = jnp.where(same_head, hh[p][:LANES] + yz[p][L:, LANES:], 0.0)
        y = hh[p][LANES:] + yz[p][:L, LANES:]
        mean = seg_sum(y) * (1.0 / half)
        yc = y - mean
        var = seg_sum(yc * yc) * (1.0 / half)
        yn = yc * lax.rsqrt(var + GN_EPS) * lnw_ref[:, cols[p]] + lnb_ref[:, cols[p]]
        bonus = seg_sum(r[p] * k[p] * rk_ref[:, cols[p]]) * v[p]
        out = yn + bonus
        o_ref[:, cols[p]] = (out[:valid] * gate_ref[:, cols[p]]).astype(o_ref.dtype)

    @pl.when(c_idx == pl.num_programs(1) - 1)
    def _():
        for p in range(n_pairs):
            h_last = h_ref[p]
            s_out_ref[p] = h_last[:, :half] + h_last[:, half:]


def _wkv(rkv, lw, al, gate, k_k, k_a, r_k, ln_w, ln_b, state, *, n_streams, n_chunks, rows, first_block):
    d = lw.shape[1]
    n_pairs = d // LANES
    s_in = jnp.swapaxes(state, -1, -2).reshape(n_streams, n_pairs, LANES, RWKV_HEAD)

    def block(b, c):
        return first_block + b * n_chunks + c

    seq = pl.BlockSpec((rows, d), lambda b, c: (block(b, c), 0))
    rkv_specs = [pl.BlockSpec((None, rows, d), lambda b, c, s=s: (s, block(b, c), 0)) for s in range(3)]
    vec = pl.BlockSpec((1, d), lambda b, c: (0, 0))
    st = pl.BlockSpec((None, n_pairs, LANES, RWKV_HEAD), lambda b, c: (b, 0, 0, 0))
    o, s_out = pl.pallas_call(
        _wkv_kernel,
        grid=(n_streams, n_chunks),
        in_specs=rkv_specs + [seq] * 3 + [vec] * 5 + [st],
        out_specs=[pl.BlockSpec((rows, d), lambda b, c: (b * n_chunks + c, 0)), st],
        out_shape=[jax.ShapeDtypeStruct((n_streams * n_chunks * rows, d), BF16),
                   jax.ShapeDtypeStruct(s_in.shape, F32)],
        scratch_shapes=[pltpu.VMEM((n_pairs, LANES, LANES), F32)],
        compiler_params=_params("parallel", "arbitrary"),
        name="rwkv7_wkv",
    )(rkv, rkv, rkv, lw, al, gate, *[x.reshape(1, d) for x in (k_k, k_a, r_k, ln_w, ln_b)], s_in)
    s_out = jnp.swapaxes(s_out.reshape(n_streams, d // RWKV_HEAD, RWKV_HEAD, RWKV_HEAD), -1, -2)
    return o, s_out


def _pad_lora(w_in, w_out):
    rank = w_in.shape[1]
    pad = (-rank) % LORA_PAD
    return (jnp.pad(w_in, ((0, 0), (0, pad))).astype(BF16),
            jnp.pad(w_out, ((0, pad), (0, 0))).astype(BF16))


def kernel(x_prompt, x_sample, cache_k, cache_v, state_shift, state_wkv, norm_g, ffn_w_gate, ffn_w_up, ffn_w_down, rel_table, att_w_qkv, att_b_qkv, att_w_o, att_sinks, rwkv_mu, rwkv_w_r, rwkv_w_k, rwkv_w_v, rwkv_w_o, rwkv_w0, rwkv_w1, rwkv_w2, rwkv_a0, rwkv_a1, rwkv_a2, rwkv_g1, rwkv_g2, rwkv_k_k, rwkv_k_a, rwkv_r_k, rwkv_ln_w, rwkv_ln_b):
    n_bp, t_p, d = x_prompt.shape
    n_bs, t_s, _ = x_sample.shape
    n_p = n_bp * t_p
    n_s = n_bs * t_s
    depth = norm_g.shape[0]
    q_dim = att_w_o.shape[1]
    kv_dim = N_KV_HEADS * HEAD_DIM
    assert n_p % ROW_TILE == 0 and n_s % ROW_TILE == 0 and ROW_TILE % t_s == 0 and t_p % CHUNK == 0
    assert t_p % WKV_CHUNK == 0 and t_s <= WKV_CHUNK and n_p % t_s == 0

    wg = ffn_w_gate.astype(BF16)
    wu = ffn_w_up.astype(BF16)
    wd = ffn_w_down.astype(BF16)

    x = jnp.concatenate([x_prompt.reshape(n_p, d), x_sample.reshape(n_s, d)], axis=0)
    new_k, new_v, new_shift, new_wkv = [], [], [], []
    for i in range(depth):
        g = norm_g[i]
        j = i // 2
        x = _ffn_block(x, g[0], g[1], wg[i, 0], wu[i, 0], wd[i, 0])
        if i % 2 == 0:
            q, k, v = _qkv_proj(x, g[2], att_w_qkv[j].astype(BF16), att_b_qkv[j], q_dim, kv_dim)

            n_back = WINDOW // CHUNK
            n_c = t_p // CHUNK
            parts = [(k, v, CHUNK, (lambda b, c, back=back: (b * n_c + jnp.maximum(c - back, 0), 0)), back)
                     for back in range(n_back, -1, -1)]
            o_p = _attention(q, parts, _rel_bias(rel_table, CHUNK, (n_back + 1) * CHUNK), att_sinks[j],
                             grid=(n_bp, n_c), q_rows=CHUNK, q_map=lambda b, c: (b * n_c + c, 0),
                             n_out_rows=n_p)
            first = n_p // t_s
            parts = [(cache_k[j].reshape(n_bs, WINDOW, kv_dim), cache_v[j].reshape(n_bs, WINDOW, kv_dim),
                      WINDOW, (lambda b, c: (b, 0, 0)), 0),
                     (k, v, t_s, (lambda b, c: (first + b, 0)), 0)]
            o_s = _attention(q, parts, _rel_bias(rel_table, t_s, WINDOW + t_s), att_sinks[j],
                             grid=(n_bs, 1), q_rows=t_s, q_map=lambda b, c: (first + b, 0),
                             n_out_rows=n_s)
            x = _out_proj(o_p, o_s, att_w_o[j].astype(BF16), g[3], x)

            def kv_rows(a):
                return (a[:n_p].reshape(n_bp, t_p, N_KV_HEADS, HEAD_DIM)[:, t_p - WINDOW:],
                        a[n_p:].reshape(n_bs, t_s, N_KV_HEADS, HEAD_DIM))

            new_k.append(kv_rows(k))
            new_v.append(kv_rows(v))
        else:
            first_rows = jnp.broadcast_to(state_shift[j], (n_bs, t_s, d)).reshape(n_s, d)
            h, hp = _norm_shift(x, g[2], first_rows, n_p, t_p, t_s)
            mu = rwkv_mu[j]
            rkv = _rkv_proj(h, hp, jnp.stack([mu[0], mu[2], mu[3]]),
                            jnp.stack([rwkv_w_r[j], rwkv_w_k[j], rwkv_w_v[j]]).astype(BF16))
            w1, w2 = _pad_lora(rwkv_w1[j], rwkv_w2[j])
            a1, a2 = _pad_lora(rwkv_a1[j], rwkv_a2[j])
            g1, g2 = _pad_lora(rwkv_g1[j], rwkv_g2[j])
            lw, al, gate = _lora_branches(h, hp, jnp.stack([mu[1], mu[4], mu[5]]), rwkv_w0[j], w1, w2,
                                          rwkv_a0[j], a1, a2, g1, g2)
            vecs = (rwkv_k_k[j], rwkv_k_a[j], rwkv_r_k[j].reshape(d), rwkv_ln_w[j], rwkv_ln_b[j])
            zero_state = jnp.zeros((n_bp, d // RWKV_HEAD, RWKV_HEAD, RWKV_HEAD), F32)
            o_p, wkv_p = _wkv(rkv, lw, al, gate, *vecs, zero_state, n_streams=n_bp,
                              n_chunks=t_p // WKV_CHUNK, rows=WKV_CHUNK, first_block=0)
            o_s, wkv_s = _wkv(rkv, lw, al, gate, *vecs, state_wkv[j], n_streams=n_bs,
                              n_chunks=1, rows=t_s, first_block=n_p // t_s)
            x = _out_proj(o_p, o_s, rwkv_w_o[j].astype(BF16), g[3], x)
            new_shift.append((h[:n_p].reshape(n_bp, t_p, d)[:, t_p - 1:],
                              h[n_p:].reshape(n_bs, t_s, d)[:, t_s - 1:]))
            new_wkv.append((wkv_p, wkv_s))
        x = _ffn_block(x, g[4], g[5], wg[i, 1], wu[i, 1], wd[i, 1])

    def both(pairs):
        return jnp.stack([p[0] for p in pairs]), jnp.stack([p[1] for p in pairs])

    k_prompt, k_sample = both(new_k)
    v_prompt, v_sample = both(new_v)
    shift_prompt, shift_sample = both(new_shift)
    wkv_prompt, wkv_sample = both(new_wkv)
    return (x[:n_p].reshape(n_bp, t_p, d), x[n_p:].reshape(n_bs, t_s, d),
            k_prompt, v_prompt, k_sample, v_sample,
            shift_prompt, wkv_prompt, shift_sample, wkv_sample)
```

```python
import functools
import math

import jax
import jax.numpy as jnp
from jax import lax
from jax.experimental import pallas as pl
from jax.experimental.pallas import tpu as pltpu

F32 = jnp.float32
BF16 = jnp.bfloat16

HEAD_DIM = 64
N_KV_HEADS = 4
CHUNK = 64
WINDOW = 128
N_BUCKETS = 32
MAX_DISTANCE = 128
RWKV_HEAD = 64
RMS_EPS = 1e-6
GN_EPS = RWKV_HEAD * 1e-5
NEG_INF = -1e30
LORA_PAD = 128

LANES = 128
KEY_PAD = 256
VMEM_LIMIT_BYTES = 56 * 1024 * 1024
ROW_TILE = 512
FF_TILE = 512
WKV_CHUNK = 64


def _params(*semantics):
    return pltpu.CompilerParams(dimension_semantics=semantics,
                                vmem_limit_bytes=VMEM_LIMIT_BYTES)


def _rms(x):
    return x * lax.rsqrt(jnp.mean(jnp.square(x), axis=-1, keepdims=True) + RMS_EPS)


def _dot(a, b):
    return jnp.dot(a, b, preferred_element_type=F32)


def _dot_nt(a, b):
    return lax.dot_general(a, b, (((1,), (1,)), ((), ())), preferred_element_type=F32)


def _two_part_specs(d, prompt_tiles):
    return [pl.BlockSpec((ROW_TILE, d), lambda i, *_: (jnp.minimum(i, prompt_tiles - 1), 0)),
            pl.BlockSpec((ROW_TILE, d), lambda i, *_: (jnp.maximum(i - prompt_tiles, 0), 0))]


def _ffn_kernel(x_ref, g0_ref, g1_ref, wg_ref, wu_ref, wd_ref, o_ref, h_ref):
    f = pl.program_id(1)

    @pl.when(f == 0)
    def _():
        h_ref[...] = (_rms(x_ref[...]) * g0_ref[...]).astype(BF16)
        o_ref[...] = jnp.zeros_like(o_ref)

    h = h_ref[...]
    a = _dot(h, wg_ref[...])
    b = _dot(h, wu_ref[...])
    act = (a * jax.nn.sigmoid(a) * b).astype(BF16)
    o_ref[...] += _dot(act, wd_ref[...])

    @pl.when(f == pl.num_programs(1) - 1)
    def _():
        o_ref[...] = x_ref[...] + 0.5 * (_rms(o_ref[...]) * g1_ref[...])


def _ffn_block(x, g0, g1, wg, wu, wd):
    n, d = x.shape
    d_ff = wg.shape[1]
    return pl.pallas_call(
        _ffn_kernel,
        grid=(n // ROW_TILE, d_ff // FF_TILE),
        in_specs=[
            pl.BlockSpec((ROW_TILE, d), lambda i, f: (i, 0)),
            pl.BlockSpec((1, d), lambda i, f: (0, 0)),
            pl.BlockSpec((1, d), lambda i, f: (0, 0)),
            pl.BlockSpec((d, FF_TILE), lambda i, f: (0, f)),
            pl.BlockSpec((d, FF_TILE), lambda i, f: (0, f)),
            pl.BlockSpec((FF_TILE, d), lambda i, f: (f, 0)),
        ],
        out_specs=pl.BlockSpec((ROW_TILE, d), lambda i, f: (i, 0)),
        out_shape=jax.ShapeDtypeStruct((n, d), F32),
        scratch_shapes=[pltpu.VMEM((ROW_TILE, d), BF16)],
        compiler_params=_params("parallel", "arbitrary"),
        name="ffn_block",
    )(x, g0.reshape(1, d), g1.reshape(1, d), wg, wu, wd)


def _norm_shift_kernel(x_ref, g_ref, first_ref, h_ref, hp_ref, carry_ref, *, prompt_tiles, prompt_len,
                       sample_len):
    i = pl.program_id(0)

    @pl.when(i == 0)
    def _():
        carry_ref[...] = jnp.zeros_like(carry_ref)

    h = _rms(x_ref[...]) * g_ref[...]
    h_ref[...] = h
    rows = lax.broadcasted_iota(jnp.int32, (ROW_TILE, 1), 0)
    prev = jnp.where(rows == 0, carry_ref[...], pltpu.roll(h, 1, axis=0))

    @pl.when(i < prompt_tiles)
    def _():
        hp_ref[...] = jnp.where((rows + i * ROW_TILE) % prompt_len == 0, 0.0, prev)

    @pl.when(i >= prompt_tiles)
    def _():
        hp_ref[...] = jnp.where(rows % sample_len == 0, first_ref[...], prev)

    carry_ref[...] = h[ROW_TILE - 1:ROW_TILE]


def _norm_shift(x, g, first_rows, n_prompt_rows, prompt_len, sample_len):
    n, d = x.shape
    prompt_tiles = n_prompt_rows // ROW_TILE
    row = pl.BlockSpec((ROW_TILE, d), lambda i: (i, 0))
    out = jax.ShapeDtypeStruct((n, d), F32)
    return pl.pallas_call(
        functools.partial(_norm_shift_kernel, prompt_tiles=prompt_tiles, prompt_len=prompt_len,
                          sample_len=sample_len),
        grid=(n // ROW_TILE,),
        in_specs=[row, pl.BlockSpec((1, d), lambda i: (0, 0)),
                  pl.BlockSpec((ROW_TILE, d), lambda i: (jnp.maximum(i - prompt_tiles, 0), 0))],
        out_specs=[row, row],
        out_shape=[out, out],
        scratch_shapes=[pltpu.VMEM((1, d), F32)],
        compiler_params=_params("arbitrary"),
        name="norm_shift",
    )(x, g.reshape(1, d), first_rows)


def _qkv_kernel(x_ref, g_ref, w_ref, b_ref, q_ref, k_ref, v_ref):
    h = (_rms(x_ref[...]) * g_ref[...]).astype(BF16)
    y = _dot(h, w_ref[...]) + b_ref[...]
    q_dim = q_ref.shape[1]
    kv_dim = k_ref.shape[1]
    q_ref[...] = y[:, :q_dim].astype(q_ref.dtype)
    k_ref[...] = y[:, q_dim:q_dim + kv_dim]
    v_ref[...] = y[:, q_dim + kv_dim:]


def _qkv_proj(x, g, w, b, q_dim, kv_dim):
    n, d = x.shape
    m = w.shape[1]

    def rows(width):
        return pl.BlockSpec((ROW_TILE, width), lambda i: (i, 0))

    return pl.pallas_call(
        _qkv_kernel,
        grid=(n // ROW_TILE,),
        in_specs=[rows(d), pl.BlockSpec((1, d), lambda i: (0, 0)),
                  pl.BlockSpec((d, m), lambda i: (0, 0)), pl.BlockSpec((1, m), lambda i: (0, 0))],
        out_specs=[rows(q_dim), rows(kv_dim), rows(kv_dim)],
        out_shape=[jax.ShapeDtypeStruct((n, q_dim), BF16), jax.ShapeDtypeStruct((n, kv_dim), F32),
                   jax.ShapeDtypeStruct((n, kv_dim), F32)],
        compiler_params=_params("parallel"),
        name="qkv_proj",
    )(x, g.reshape(1, d), w, b.reshape(1, m))


def _rkv_kernel(h_ref, hp_ref, mu_ref, w_ref, o_ref):
    h = h_ref[...]
    x = (h + (hp_ref[...] - h) * mu_ref[...]).astype(BF16)
    o_ref[...] = _dot(x, w_ref[...])


def _rkv_proj(h, hp, mu, w):
    n, d = h.shape
    k = w.shape[0]
    m = w.shape[2]
    return pl.pallas_call(
        _rkv_kernel,
        grid=(n // ROW_TILE, k),
        in_specs=[pl.BlockSpec((ROW_TILE, d), lambda i, s: (i, 0)),
                  pl.BlockSpec((ROW_TILE, d), lambda i, s: (i, 0)),
                  pl.BlockSpec((None, 1, d), lambda i, s: (s, 0, 0)),
                  pl.BlockSpec((None, d, m), lambda i, s: (s, 0, 0))],
        out_specs=pl.BlockSpec((None, ROW_TILE, m), lambda i, s: (s, i, 0)),
        out_shape=jax.ShapeDtypeStruct((k, n, m), F32),
        compiler_params=_params("parallel", "arbitrary"),
        name="rkv_proj",
    )(h, hp, mu.reshape(k, 1, d), w)


def _lora_kernel(h_ref, hp_ref, mu_ref, w0_ref, w1_ref, w2_ref, a0_ref, a1_ref, a2_ref,
                 g1_ref, g2_ref, lw_ref, al_ref, gate_ref):
    h = h_ref[...]
    dx = hp_ref[...] - h

    xw = (h + dx * mu_ref[0]).astype(BF16)
    t = jnp.tanh(_dot(xw, w1_ref[...])).astype(BF16)
    z = -(w0_ref[...] + _dot(t, w2_ref[...]))
    softplus = jnp.maximum(z, 0.0) + jnp.log(1.0 + jnp.exp(-jnp.abs(z)))
    lw_ref[...] = -jnp.exp(-softplus - 0.5)

    xa = (h + dx * mu_ref[1]).astype(BF16)
    u = _dot(xa, a1_ref[...]).astype(BF16)
    al_ref[...] = jax.nn.sigmoid(a0_ref[...] + _dot(u, a2_ref[...]))

    xg = (h + dx * mu_ref[2]).astype(BF16)
    s = jax.nn.sigmoid(_dot(xg, g1_ref[...])).astype(BF16)
    gate_ref[...] = _dot(s, g2_ref[...])


def _lora_branches(h, hp, mu, w0, w1, w2, a0, a1, a2, g1, g2):
    n, d = h.shape
    row = pl.BlockSpec((ROW_TILE, d), lambda i: (i, 0))

    def whole(x):
        return pl.BlockSpec(x.shape, lambda i: (0,) * x.ndim)

    mu = mu.reshape(3, 1, d)
    w0 = w0.reshape(1, d)
    a0 = a0.reshape(1, d)
    consts = (mu, w0, w1, w2, a0, a1, a2, g1, g2)
    out = jax.ShapeDtypeStruct((n, d), F32)
    return pl.pallas_call(
        _lora_kernel,
        grid=(n // ROW_TILE,),
        in_specs=[row, row] + [whole(c) for c in consts],
        out_specs=[row, row, row],
        out_shape=[out, out, out],
        compiler_params=_params("parallel"),
        name="rwkv_lora",
    )(h, hp, *consts)


def _out_proj_kernel(op_ref, os_ref, w_ref, g_ref, x_ref, y_ref, *, prompt_tiles):
    o = jnp.where(pl.program_id(0) < prompt_tiles, op_ref[...], os_ref[...])
    y = _dot(o, w_ref[...])
    y_ref[...] = x_ref[...] + _rms(y) * g_ref[...]


def _out_proj(o_prompt, o_sample, w, g, xres):
    n, d = xres.shape
    k = o_prompt.shape[1]
    prompt_tiles = o_prompt.shape[0] // ROW_TILE
    return pl.pallas_call(
        functools.partial(_out_proj_kernel, prompt_tiles=prompt_tiles),
        grid=(n // ROW_TILE,),
        in_specs=_two_part_specs(k, prompt_tiles) + [
            pl.BlockSpec((k, d), lambda i: (0, 0)),
            pl.BlockSpec((1, d), lambda i: (0, 0)),
            pl.BlockSpec((ROW_TILE, d), lambda i: (i, 0))],
        out_specs=pl.BlockSpec((ROW_TILE, d), lambda i: (i, 0)),
        out_shape=jax.ShapeDtypeStruct((n, d), F32),
        compiler_params=_params("arbitrary"),
        name="out_proj",
    )(o_prompt, o_sample, w, g.reshape(1, d), xres)


def _t5_bucket(rel):
    nb = N_BUCKETS // 2
    max_exact = nb // 2
    offset = jnp.where(rel > 0, nb, 0)
    n = jnp.abs(rel)
    nf = jnp.maximum(n, 1).astype(F32)
    large = max_exact + (jnp.log(nf / max_exact) / math.log(MAX_DISTANCE / max_exact)
                         * (nb - max_exact)).astype(jnp.int32)
    large = jnp.minimum(large, nb - 1)
    return offset + jnp.where(n < max_exact, n, large)


def _bias_kernel(tt_ref, b_ref, o_ref):
    onehot = (lax.broadcasted_iota(jnp.int32, (tt_ref.shape[1], b_ref.shape[1]), 0) == b_ref[...]).astype(F32)
    o_ref[...] = jnp.dot(tt_ref[...], onehot, preferred_element_type=F32,
                         precision=lax.Precision.HIGHEST)


def _logit_offsets(table, sinks, n_q, n_k):
    assert n_k < KEY_PAD
    rel = (jnp.arange(n_k, dtype=jnp.int32)[None, :] - WINDOW - jnp.arange(n_q, dtype=jnp.int32)[:, None])
    rows = jnp.concatenate([_t5_bucket(rel).astype(jnp.int32),
                            jnp.full((n_q, 1), N_BUCKETS, jnp.int32),
                            jnp.full((n_q, KEY_PAD - n_k - 1), N_BUCKETS + 1, jnp.int32)], axis=1)
    n_heads = table.shape[1]
    values = jnp.concatenate([table.T, sinks.astype(F32)[:, None],
                              jnp.full((n_heads, 1), NEG_INF, F32)], axis=1)
    out = pl.pallas_call(
        _bias_kernel,
        out_shape=jax.ShapeDtypeStruct((n_heads, n_q * KEY_PAD), F32),
        name="logit_offsets",
    )(values, rows.reshape(1, n_q * KEY_PAD))
    return out.reshape(n_heads, n_q, KEY_PAD)


def _attn_kernel(*refs, n_parts, group, first_valid_step):
    q_ref = refs[0]
    k_refs = refs[1:1 + n_parts]
    v_refs = refs[1 + n_parts:1 + 2 * n_parts]
    off_ref, o_ref = refs[1 + 2 * n_parts:]
    n_q = q_ref.shape[0]
    step = pl.program_id(1)
    scale = HEAD_DIM ** -0.5
    part_rows = [r.shape[0] for r in k_refs]
    n_k = sum(part_rows)
    heads = range(N_KV_HEADS)
    tiles = range(group // 2)

    col = lax.broadcasted_iota(jnp.int32, (1, KEY_PAD), 1)
    hidden = None
    start = 0
    for p, rows in enumerate(part_rows):
        if first_valid_step[p] > 0:
            h_p = (col >= start) & (col < start + rows) & (step < first_valid_step[p])
            hidden = h_p if hidden is None else hidden | h_p
        start += rows

    zeros = jnp.zeros((KEY_PAD, HEAD_DIM), F32)
    ones = jnp.ones((KEY_PAD, HEAD_DIM), F32)

    def head_rows(refs_, hk):
        kv_cols = slice(hk * HEAD_DIM, (hk + 1) * HEAD_DIM)
        return jnp.concatenate([r[:, kv_cols] for r in refs_] + [zeros[:KEY_PAD - n_k]], axis=0)

    scores = []
    for hk in heads:
        k = head_rows(k_refs, hk)
        k2 = jnp.concatenate([jnp.concatenate([k, zeros], axis=1),
                              jnp.concatenate([zeros, k], axis=1)], axis=0).astype(BF16)
        q_rows = jnp.concatenate(
            [q_ref[:, (hk * group + 2 * t) * HEAD_DIM:(hk * group + 2 * t + 2) * HEAD_DIM] for t in tiles],
            axis=0)
        off = jnp.concatenate(
            [jnp.concatenate([off_ref[hk * group + 2 * t], off_ref[hk * group + 2 * t + 1]], axis=1)
             for t in tiles], axis=0)
        s = _dot_nt(q_rows, k2) * scale + off
        if hidden is not None:
            s = jnp.where(jnp.concatenate([hidden, hidden], axis=1), NEG_INF, s)
        scores.append(s)

    expo = []
    for hk in heads:
        s = scores[hk]
        m0 = jnp.max(s[:, :KEY_PAD], axis=-1, keepdims=True)
        m1 = jnp.max(s[:, KEY_PAD:], axis=-1, keepdims=True)
        expo.append(jnp.concatenate([jnp.exp(s[:, :KEY_PAD] - m0), jnp.exp(s[:, KEY_PAD:] - m1)],
                                    axis=1).astype(BF16))

    for hk in heads:
        v = head_rows(v_refs, hk)
        rhs = jnp.concatenate([jnp.concatenate([v, zeros, ones, zeros], axis=1),
                               jnp.concatenate([zeros, v, zeros, ones], axis=1)], axis=0).astype(BF16)
        acc = _dot(expo[hk], rhs)
        o = acc[:, :LANES] / acc[:, LANES:]
        for t in tiles:
            first_head = hk * group + 2 * t
            o_ref[:, first_head * HEAD_DIM:(first_head + 2) * HEAD_DIM] = (
                o[t * n_q:(t + 1) * n_q].astype(o_ref.dtype))


def _attention(q, parts, offsets, *, grid, q_rows, q_map, n_out_rows):
    q_dim = q.shape[1]
    n_heads = q_dim // HEAD_DIM
    group = n_heads // N_KV_HEADS
    assert offsets.shape == (n_heads, q_rows, KEY_PAD) and group % 2 == 0

    def kv_spec(arr, rows, imap):
        if arr.ndim == 3:
            return pl.BlockSpec((None, rows, arr.shape[2]), imap)
        return pl.BlockSpec((rows, arr.shape[1]), imap)

    k_specs = [kv_spec(p[0], p[2], p[3]) for p in parts]
    v_specs = [kv_spec(p[1], p[2], p[3]) for p in parts]
    n_steps = grid[1]
    return pl.pallas_call(
        functools.partial(_attn_kernel, n_parts=len(parts), group=group,
                          first_valid_step=tuple(p[4] for p in parts)),
        grid=grid,
        in_specs=[pl.BlockSpec((q_rows, q_dim), q_map)] + k_specs + v_specs + [
            pl.BlockSpec(offsets.shape, lambda b, c: (0, 0, 0))],
        out_specs=pl.BlockSpec((q_rows, q_dim), lambda b, c: (b * n_steps + c, 0)),
        out_shape=jax.ShapeDtypeStruct((n_out_rows, q_dim), BF16),
        compiler_params=_params("parallel", "arbitrary"),
        name="swa_attention",
    )(q, *[p[0] for p in parts], *[p[1] for p in parts], offsets)


def _split3(x):
    hi = x.astype(BF16)
    r1 = x - hi.astype(F32)
    mid = r1.astype(BF16)
    lo = (r1 - mid.astype(F32)).astype(BF16)
    return hi, mid, lo


def _wkv_kernel(r_ref, k_ref, v_ref, lw_ref, al_ref, gate_ref, kk_ref, ka_ref, rk_ref, lnw_ref,
                lnb_ref, s0_ref, o_ref, s_out_ref, h_ref):
    L = WKV_CHUNK
    L2 = 2 * L
    valid = r_ref.shape[0]
    n_pairs = r_ref.shape[1] // LANES
    c_idx = pl.program_id(1)
    half = RWKV_HEAD

    lane = lax.broadcasted_iota(jnp.int32, (1, LANES), 1)
    head0 = lane < half
    row2 = lax.broadcasted_iota(jnp.int32, (L2, L2), 0)
    col2 = lax.broadcasted_iota(jnp.int32, (L2, L2), 1)
    strict_lower = row2 > col2
    lower = row2 >= col2
    eye2 = (row2 == col2).astype(F32)
    tril = (lax.broadcasted_iota(jnp.int32, (L, L), 0) >= lax.broadcasted_iota(jnp.int32, (L, L), 1)
            ).astype(BF16)
    krow = lax.broadcasted_iota(jnp.int32, (LANES, LANES), 0)
    kcol = lax.broadcasted_iota(jnp.int32, (LANES, LANES), 1)
    same_head = (krow < half) == (kcol < half)
    eye_k = (krow == kcol).astype(F32)

    def load(ref, cs):
        x = ref[:, cs]
        if valid < L:
            x = jnp.concatenate([x, jnp.zeros((L - valid, LANES), x.dtype)], axis=0)
        return x

    def seg_sum(x):
        s0 = jnp.sum(jnp.where(head0, x, 0.0), axis=-1, keepdims=True)
        s1 = jnp.sum(jnp.where(head0, 0.0, x), axis=-1, keepdims=True)
        return jnp.where(head0, s0, s1)

    def stack2(x):
        return jnp.concatenate([jnp.where(head0, x, 0.0), jnp.where(head0, 0.0, x)], axis=0)

    def fold(m):
        return m[:L] + m[L:]

    @pl.when(c_idx == 0)
    def _():
        for p in range(n_pairs):
            s0 = jnp.concatenate([s0_ref[2 * p].T, s0_ref[2 * p + 1].T], axis=0)
            h_ref[p] = jnp.where(same_head, jnp.concatenate([s0, s0], axis=1), 0.0)

    pairs = range(n_pairs)
    cols = [slice(p * LANES, (p + 1) * LANES) for p in pairs]

    r = [load(r_ref, cs) for cs in cols]
    v = [load(v_ref, cs) for cs in cols]
    lw = [load(lw_ref, cs) for cs in cols]
    c3 = [_dot(tril, jnp.concatenate(_split3(x), axis=1)) for x in lw]
    c = [x[:, :LANES] + x[:, LANES:2 * LANES] + x[:, 2 * LANES:] for x in c3]

    k, a, b = [], [], []
    for p in pairs:
        k_raw = load(k_ref, cols[p])
        al = load(al_ref, cols[p])
        kk = k_raw * kk_ref[:, cols[p]]
        kk = kk / jnp.maximum(jnp.sqrt(seg_sum(kk * kk)), 1e-12)
        k.append(k_raw * (1.0 + (al - 1.0) * ka_ref[:, cols[p]]))
        a.append(-kk)
        b.append(kk * al)

    at2, v2, rt, bc2, kc2, d_end, gram = [], [], [], [], [], [], []
    for p in pairs:
        c_last = c[p][L - 1:L]
        e_neg = jnp.exp(-c[p])
        e_end = jnp.exp(c_last - c[p])
        at2.append(stack2(a[p] * jnp.exp(c[p] - lw[p])))
        rt.append(r[p] * jnp.exp(c[p]))
        v2.append(stack2(v[p]))
        bc2.append(stack2(b[p] * e_end))
        kc2.append(stack2(k[p] * e_end))
        d_end.append(jnp.exp(c_last))
        gram.append(_dot_nt(
            jnp.concatenate([at2[p], stack2(rt[p])], axis=0).astype(BF16),
            jnp.concatenate([stack2(b[p] * e_neg), stack2(k[p] * e_neg)], axis=0).astype(BF16)))

    n_mat = [jnp.where(strict_lower, g[:L2, :L2], 0.0) for g in gram]
    akv2 = [_dot(jnp.where(strict_lower, gram[p][:L2, L2:], 0.0).astype(BF16), v2[p].astype(BF16))
            for p in pairs]
    lhs_y = [jnp.concatenate([fold(jnp.where(lower, g[L2:, :L2], 0.0)),
                              fold(jnp.where(lower, g[L2:, L2:], 0.0))], axis=1) for g in gram]

    size = 1
    t_inv = [eye2 for _ in pairs]
    while size < L:
        quadrant = ((row2 // (2 * size)) == (col2 // (2 * size))) & \
                   ((row2 // size) % 2 == 1) & ((col2 // size) % 2 == 0)
        n_c = [jnp.where(quadrant, n, 0.0) for n in n_mat]
        if size == 1:
            t_inv = [t + n for t, n in zip(t_inv, n_c)]
        else:
            t_bf = [t.astype(BF16) for t in t_inv]
            tn = [_dot(t, n.astype(BF16)).astype(BF16) for t, n in zip(t_bf, n_c)]
            t_inv = [t + _dot(x, tb) for t, x, tb in zip(t_inv, tn, t_bf)]
        size *= 2

    u2 = [_dot(t_inv[p].astype(BF16), jnp.concatenate([at2[p], akv2[p]], axis=1).astype(BF16))
          for p in pairs]
    yz = []
    for p in pairs:
        z = jnp.concatenate(
            [u2[p], jnp.concatenate([jnp.zeros((L2, LANES), F32), v2[p]], axis=1)], axis=0).astype(BF16)
        lhs_h = jnp.concatenate([bc2[p], kc2[p]], axis=0).T
        yz.append(_dot(jnp.concatenate([lhs_y[p], lhs_h], axis=0).astype(BF16), z))

    hh = []
    for p in pairs:
        q_hat = rt[p] + yz[p][:L, :LANES]
        m_mat = eye_k * d_end[p] + yz[p][L:, :LANES]
        hh.append(_dot(jnp.concatenate([m_mat, q_hat], axis=0).astype(BF16), h_ref[p].astype(BF16)))

    for p in pairs:
        h_ref[p] = jnp.where(same_head, hh[p][:LANES] + yz[p][L:, LANES:], 0.0)
        y = hh[p][LANES:] + yz[p][:L, LANES:]
        mean = seg_sum(y) * (1.0 / half)
        yc = y - mean
        var = seg_sum(yc * yc) * (1.0 / half)
        yn = yc * lax.rsqrt(var + GN_EPS) * lnw_ref[:, cols[p]] + lnb_ref[:, cols[p]]
        bonus = seg_sum(r[p] * k[p] * rk_ref[:, cols[p]]) * v[p]
        out = yn + bonus
        o_ref[:, cols[p]] = (out[:valid] * gate_ref[:, cols[p]]).astype(o_ref.dtype)

    @pl.when(c_idx == pl.num_programs(1) - 1)
    def _():
        for p in range(n_pairs):
            h_last = h_ref[p]
            s_out_ref[2 * p] = h_last[:half, :half].T
            s_out_ref[2 * p + 1] = h_last[half:, half:].T


def _wkv(rkv, lw, al, gate, k_k, k_a, r_k, ln_w, ln_b, state, *, n_streams, n_chunks, rows, first_block):
    d = lw.shape[1]
    n_pairs = d // LANES

    def block(b, c):
        return first_block + b * n_chunks + c

    seq = pl.BlockSpec((rows, d), lambda b, c: (block(b, c), 0))
    rkv_specs = [pl.BlockSpec((None, rows, d), lambda b, c, s=s: (s, block(b, c), 0)) for s in range(3)]
    vec = pl.BlockSpec((1, d), lambda b, c: (0, 0))
    st = pl.BlockSpec((None,) + state.shape[1:], lambda b, c: (b, 0, 0, 0))
    return pl.pallas_call(
        _wkv_kernel,
        grid=(n_streams, n_chunks),
        in_specs=rkv_specs + [seq] * 3 + [vec] * 5 + [st],
        out_specs=[pl.BlockSpec((rows, d), lambda b, c: (b * n_chunks + c, 0)), st],
        out_shape=[jax.ShapeDtypeStruct((n_streams * n_chunks * rows, d), BF16),
                   jax.ShapeDtypeStruct(state.shape, F32)],
        scratch_shapes=[pltpu.VMEM((n_pairs, LANES, LANES), F32)],
        compiler_params=_params("parallel", "arbitrary"),
        name="rwkv7_wkv",
    )(rkv, rkv, rkv, lw, al, gate, *[x.reshape(1, d) for x in (k_k, k_a, r_k, ln_w, ln_b)], state)


def _pad_lora(w_in, w_out):
    rank = w_in.shape[1]
    pad = (-rank) % LORA_PAD
    return (jnp.pad(w_in, ((0, 0), (0, pad))).astype(BF16),
            jnp.pad(w_out, ((0, pad), (0, 0))).astype(BF16))


def kernel(x_prompt, x_sample, cache_k, cache_v, state_shift, state_wkv, norm_g, ffn_w_gate, ffn_w_up, ffn_w_down, rel_table, att_w_qkv, att_b_qkv, att_w_o, att_sinks, rwkv_mu, rwkv_w_r, rwkv_w_k, rwkv_w_v, rwkv_w_o, rwkv_w0, rwkv_w1, rwkv_w2, rwkv_a0, rwkv_a1, rwkv_a2, rwkv_g1, rwkv_g2, rwkv_k_k, rwkv_k_a, rwkv_r_k, rwkv_ln_w, rwkv_ln_b):
    n_bp, t_p, d = x_prompt.shape
    n_bs, t_s, _ = x_sample.shape
    n_p = n_bp * t_p
    n_s = n_bs * t_s
    depth = norm_g.shape[0]
    q_dim = att_w_o.shape[1]
    kv_dim = N_KV_HEADS * HEAD_DIM
    assert n_p % ROW_TILE == 0 and n_s % ROW_TILE == 0 and ROW_TILE % t_s == 0 and t_p % CHUNK == 0
    assert t_p % WKV_CHUNK == 0 and t_s <= WKV_CHUNK and n_p % t_s == 0

    wg = ffn_w_gate.astype(BF16)
    wu = ffn_w_up.astype(BF16)
    wd = ffn_w_down.astype(BF16)

    x = jnp.concatenate([x_prompt.reshape(n_p, d), x_sample.reshape(n_s, d)], axis=0)
    new_k, new_v, new_shift, new_wkv = [], [], [], []
    for i in range(depth):
        g = norm_g[i]
        j = i // 2
        x = _ffn_block(x, g[0], g[1], wg[i, 0], wu[i, 0], wd[i, 0])
        if i % 2 == 0:
            q, k, v = _qkv_proj(x, g[2], att_w_qkv[j].astype(BF16), att_b_qkv[j], q_dim, kv_dim)

            n_back = WINDOW // CHUNK
            n_c = t_p // CHUNK
            parts = [(k, v, CHUNK, (lambda b, c, back=back: (b * n_c + jnp.maximum(c - back, 0), 0)), back)
                     for back in range(n_back, -1, -1)]
            o_p = _attention(q, parts, _logit_offsets(rel_table, att_sinks[j], CHUNK, (n_back + 1) * CHUNK),
                             grid=(n_bp, n_c), q_rows=CHUNK, q_map=lambda b, c: (b * n_c + c, 0),
                             n_out_rows=n_p)
            first = n_p // t_s
            parts = [(cache_k[j].reshape(n_bs, WINDOW, kv_dim), cache_v[j].reshape(n_bs, WINDOW, kv_dim),
                      WINDOW, (lambda b, c: (b, 0, 0)), 0),
                     (k, v, t_s, (lambda b, c: (first + b, 0)), 0)]
            o_s = _attention(q, parts, _logit_offsets(rel_table, att_sinks[j], t_s, WINDOW + t_s),
                             grid=(n_bs, 1), q_rows=t_s, q_map=lambda b, c: (first + b, 0),
                             n_out_rows=n_s)
            x = _out_proj(o_p, o_s, att_w_o[j].astype(BF16), g[3], x)

            def kv_rows(a):
                return (a[:n_p].reshape(n_bp, t_p, N_KV_HEADS, HEAD_DIM)[:, t_p - WINDOW:],
                        a[n_p:].reshape(n_bs, t_s, N_KV_HEADS, HEAD_DIM))

            new_k.append(kv_rows(k))
            new_v.append(kv_rows(v))
        else:
            first_rows = jnp.broadcast_to(state_shift[j], (n_bs, t_s, d)).reshape(n_s, d)
            h, hp = _norm_shift(x, g[2], first_rows, n_p, t_p, t_s)
            mu = rwkv_mu[j]
            rkv = _rkv_proj(h, hp, jnp.stack([mu[0], mu[2], mu[3]]),
                            jnp.stack([rwkv_w_r[j], rwkv_w_k[j], rwkv_w_v[j]]).astype(BF16))
            w1, w2 = _pad_lora(rwkv_w1[j], rwkv_w2[j])
            a1, a2 = _pad_lora(rwkv_a1[j], rwkv_a2[j])
            g1, g2 = _pad_lora(rwkv_g1[j], rwkv_g2[j])
            lw, al, gate = _lora_branches(h, hp, jnp.stack([mu[1], mu[4], mu[5]]), rwkv_w0[j], w1, w2,
                                          rwkv_a0[j], a1, a2, g1, g2)
            vecs = (rwkv_k_k[j], rwkv_k_a[j], rwkv_r_k[j].reshape(d), rwkv_ln_w[j], rwkv_ln_b[j])
            zero_state = jnp.zeros((n_bp, d // RWKV_HEAD, RWKV_HEAD, RWKV_HEAD), F32)
            o_p, wkv_p = _wkv(rkv, lw, al, gate, *vecs, zero_state, n_streams=n_bp,
                              n_chunks=t_p // WKV_CHUNK, rows=WKV_CHUNK, first_block=0)
            o_s, wkv_s = _wkv(rkv, lw, al, gate, *vecs, state_wkv[j], n_streams=n_bs,
                              n_chunks=1, rows=t_s, first_block=n_p // t_s)
            x = _out_proj(o_p, o_s, rwkv_w_o[j].astype(BF16), g[3], x)
            new_shift.append((h[:n_p].reshape(n_bp, t_p, d)[:, t_p - 1:],
                              h[n_p:].reshape(n_bs, t_s, d)[:, t_s - 1:]))
            new_wkv.append((wkv_p, wkv_s))
        x = _ffn_block(x, g[4], g[5], wg[i, 1], wu[i, 1], wd[i, 1])

    def both(pairs):
        return jnp.stack([p[0] for p in pairs]), jnp.stack([p[1] for p in pairs])

    k_prompt, k_sample = both(new_k)
    v_prompt, v_sample = both(new_v)
    shift_prompt, shift_sample = both(new_shift)
    wkv_prompt, wkv_sample = both(new_wkv)
    return (x[:n_p].reshape(n_bp, t_p, d), x[n_p:].reshape(n_bs, t_s, d),
            k_prompt, v_prompt, k_sample, v_sample,
            shift_prompt, wkv_prompt, shift_sample, wkv_sample)
```

```python
import functools
import math

import jax
import jax.numpy as jnp
from jax import lax
from jax.experimental import pallas as pl
from jax.experimental.pallas import tpu as pltpu

F32 = jnp.float32
BF16 = jnp.bfloat16

HEAD_DIM = 64
N_KV_HEADS = 4
CHUNK = 64
WINDOW = 128
N_BUCKETS = 32
MAX_DISTANCE = 128
RWKV_HEAD = 64
RMS_EPS = 1e-6
GN_EPS = RWKV_HEAD * 1e-5
NEG_INF = -1e30
LORA_PAD = 128

LANES = 128
KEY_PAD = 256
VMEM_LIMIT_BYTES = 56 * 1024 * 1024
ROW_TILE = 512
RKV_ROW_TILE = 256
FF_TILE = 512
WKV_CHUNK = 64


def _params(*semantics):
    return pltpu.CompilerParams(dimension_semantics=semantics,
                                vmem_limit_bytes=VMEM_LIMIT_BYTES)


def _rms(x):
    return x * lax.rsqrt(jnp.mean(jnp.square(x), axis=-1, keepdims=True) + RMS_EPS)


def _dot(a, b):
    return jnp.dot(a, b, preferred_element_type=F32)


def _dot_nt(a, b):
    return lax.dot_general(a, b, (((1,), (1,)), ((), ())), preferred_element_type=F32)


def _two_part_specs(d, prompt_tiles):
    return [pl.BlockSpec((ROW_TILE, d), lambda i, *_: (jnp.minimum(i, prompt_tiles - 1), 0)),
            pl.BlockSpec((ROW_TILE, d), lambda i, *_: (jnp.maximum(i - prompt_tiles, 0), 0))]


def _ffn_kernel(*refs, n_x, n_out, prompt_tiles):
    x_refs = refs[:n_x]
    g0_ref, g1_ref, wg_ref, wu_ref, wd_ref = refs[n_x:n_x + 5]
    o_refs = refs[n_x + 5:n_x + 5 + n_out]
    h_ref = refs[n_x + 5 + n_out]
    i = pl.program_id(0)
    f = pl.program_id(1)

    def body(x_ref, o_ref):
        @pl.when(f == 0)
        def _():
            h_ref[...] = (_rms(x_ref[...]) * g0_ref[...]).astype(BF16)
            o_ref[...] = jnp.zeros_like(o_ref)

        h = h_ref[...]
        a = _dot(h, wg_ref[...])
        b = _dot(h, wu_ref[...])
        act = (a * jax.nn.sigmoid(a) * b).astype(BF16)
        o_ref[...] += _dot(act, wd_ref[...])

        @pl.when(f == pl.num_programs(1) - 1)
        def _():
            o_ref[...] = x_ref[...] + 0.5 * (_rms(o_ref[...]) * g1_ref[...])

    if n_x == 1 and n_out == 1:
        body(x_refs[0], o_refs[0])
    else:
        pl.when(i < prompt_tiles)(lambda: body(x_refs[0], o_refs[0]))
        pl.when(i >= prompt_tiles)(lambda: body(x_refs[-1], o_refs[-1]))


def _ffn_block(x, g0, g1, w, layer, half, split_rows=None):
    xs = x if isinstance(x, tuple) else (x,)
    d = xs[0].shape[1]
    n = sum(a.shape[0] for a in xs)
    d_ff = w[0].shape[-1]
    row = pl.BlockSpec((ROW_TILE, d), lambda i, f: (i, 0))
    prompt_tiles = n // ROW_TILE
    x_specs, out_specs, out_shape = [row], [row], [jax.ShapeDtypeStruct((n, d), F32)]
    if len(xs) == 2:
        prompt_tiles = xs[0].shape[0] // ROW_TILE
        x_specs = _two_part_specs(d, prompt_tiles)
    if split_rows is not None:
        prompt_tiles = split_rows // ROW_TILE
        out_specs = _two_part_specs(d, prompt_tiles)
        out_shape = [jax.ShapeDtypeStruct((split_rows, d), F32),
                     jax.ShapeDtypeStruct((n - split_rows, d), F32)]
    out = pl.pallas_call(
        functools.partial(_ffn_kernel, n_x=len(xs), n_out=len(out_shape), prompt_tiles=prompt_tiles),
        grid=(n // ROW_TILE, d_ff // FF_TILE),
        in_specs=x_specs + [
            pl.BlockSpec((1, d), lambda i, f: (0, 0)),
            pl.BlockSpec((1, d), lambda i, f: (0, 0)),
            pl.BlockSpec((None, None, d, FF_TILE), lambda i, f: (layer, half, 0, f)),
            pl.BlockSpec((None, None, d, FF_TILE), lambda i, f: (layer, half, 0, f)),
            pl.BlockSpec((None, None, FF_TILE, d), lambda i, f: (layer, half, f, 0)),
        ],
        out_specs=out_specs,
        out_shape=out_shape,
        scratch_shapes=[pltpu.VMEM((ROW_TILE, d), BF16)],
        compiler_params=_params("arbitrary", "arbitrary"),
        name="ffn_block",
    )(*xs, g0.reshape(1, d), g1.reshape(1, d), *w)
    return tuple(out) if split_rows is not None else out[0]


def _norm_shift_kernel(x_ref, g_ref, first_ref, h_ref, hp_ref, carry_ref, *, prompt_tiles, prompt_len,
                       sample_len):
    i = pl.program_id(0)

    @pl.when(i == 0)
    def _():
        carry_ref[...] = jnp.zeros_like(carry_ref)

    h = _rms(x_ref[...]) * g_ref[...]
    h_ref[...] = h
    rows = lax.broadcasted_iota(jnp.int32, (ROW_TILE, 1), 0)
    prev = jnp.where(rows == 0, carry_ref[...], pltpu.roll(h, 1, axis=0))

    @pl.when(i < prompt_tiles)
    def _():
        hp_ref[...] = jnp.where((rows + i * ROW_TILE) % prompt_len == 0, 0.0, prev)

    @pl.when(i >= prompt_tiles)
    def _():
        hp_ref[...] = jnp.where(rows % sample_len == 0, first_ref[...], prev)

    carry_ref[...] = h[ROW_TILE - 1:ROW_TILE]


def _norm_shift(x, g, first_rows, n_prompt_rows, prompt_len, sample_len):
    n, d = x.shape
    prompt_tiles = n_prompt_rows // ROW_TILE
    row = pl.BlockSpec((ROW_TILE, d), lambda i: (i, 0))
    out = jax.ShapeDtypeStruct((n, d), F32)
    return pl.pallas_call(
        functools.partial(_norm_shift_kernel, prompt_tiles=prompt_tiles, prompt_len=prompt_len,
                          sample_len=sample_len),
        grid=(n // ROW_TILE,),
        in_specs=[row, pl.BlockSpec((1, d), lambda i: (0, 0)),
                  pl.BlockSpec((ROW_TILE, d), lambda i: (jnp.maximum(i - prompt_tiles, 0), 0))],
        out_specs=[row, row],
        out_shape=[out, out],
        scratch_shapes=[pltpu.VMEM((1, d), F32)],
        compiler_params=_params("arbitrary"),
        name="norm_shift",
    )(x, g.reshape(1, d), first_rows)


def _qkv_kernel(x_ref, g_ref, w_ref, b_ref, q_ref, k_ref, v_ref):
    h = (_rms(x_ref[...]) * g_ref[...]).astype(BF16)
    y = _dot(h, w_ref[...]) + b_ref[...]
    q_dim = q_ref.shape[1]
    kv_dim = k_ref.shape[1]
    q_ref[...] = y[:, :q_dim].astype(q_ref.dtype)
    k_ref[...] = y[:, q_dim:q_dim + kv_dim]
    v_ref[...] = y[:, q_dim + kv_dim:]


def _qkv_proj(x, g, w, b, q_dim, kv_dim):
    n, d = x.shape
    m = w.shape[1]

    def rows(width):
        return pl.BlockSpec((ROW_TILE, width), lambda i: (i, 0))

    return pl.pallas_call(
        _qkv_kernel,
        grid=(n // ROW_TILE,),
        in_specs=[rows(d), pl.BlockSpec((1, d), lambda i: (0, 0)),
                  pl.BlockSpec((d, m), lambda i: (0, 0)), pl.BlockSpec((1, m), lambda i: (0, 0))],
        out_specs=[rows(q_dim), rows(kv_dim), rows(kv_dim)],
        out_shape=[jax.ShapeDtypeStruct((n, q_dim), BF16), jax.ShapeDtypeStruct((n, kv_dim), F32),
                   jax.ShapeDtypeStruct((n, kv_dim), F32)],
        compiler_params=_params("parallel"),
        name="qkv_proj",
    )(x, g.reshape(1, d), w, b.reshape(1, m))


def _rkv_kernel(h_ref, hp_ref, mu_ref, w_ref, o_ref):
    h = h_ref[...]
    dx = hp_ref[...] - h
    for s in range(w_ref.shape[0]):
        o_ref[s] = _dot((h + dx * mu_ref[s]).astype(BF16), w_ref[s])


def _rkv_proj(h, hp, mu, w):
    n, d = h.shape
    k = w.shape[0]
    m = w.shape[2]
    rows = pl.BlockSpec((RKV_ROW_TILE, d), lambda i: (i, 0))
    return pl.pallas_call(
        _rkv_kernel,
        grid=(n // RKV_ROW_TILE,),
        in_specs=[rows, rows,
                  pl.BlockSpec((k, 1, d), lambda i: (0, 0, 0)),
                  pl.BlockSpec((k, d, m), lambda i: (0, 0, 0), pipeline_mode=pl.Buffered(1))],
        out_specs=pl.BlockSpec((k, RKV_ROW_TILE, m), lambda i: (0, i, 0)),
        out_shape=jax.ShapeDtypeStruct((k, n, m), F32),
        compiler_params=_params("parallel"),
        name="rkv_proj",
    )(h, hp, mu.reshape(k, 1, d), w)


def _lora_kernel(h_ref, hp_ref, mu_ref, w0_ref, w1_ref, w2_ref, a0_ref, a1_ref, a2_ref,
                 g1_ref, g2_ref, lw_ref, al_ref, gate_ref):
    h = h_ref[...]
    dx = hp_ref[...] - h

    xw = (h + dx * mu_ref[0]).astype(BF16)
    t = jnp.tanh(_dot(xw, w1_ref[...])).astype(BF16)
    z = -(w0_ref[...] + _dot(t, w2_ref[...]))
    softplus = jnp.maximum(z, 0.0) + jnp.log(1.0 + jnp.exp(-jnp.abs(z)))
    lw_ref[...] = -jnp.exp(-softplus - 0.5)

    xa = (h + dx * mu_ref[1]).astype(BF16)
    u = _dot(xa, a1_ref[...]).astype(BF16)
    al_ref[...] = jax.nn.sigmoid(a0_ref[...] + _dot(u, a2_ref[...]))

    xg = (h + dx * mu_ref[2]).astype(BF16)
    s = jax.nn.sigmoid(_dot(xg, g1_ref[...])).astype(BF16)
    gate_ref[...] = _dot(s, g2_ref[...])


def _lora_branches(h, hp, mu, w0, w1, w2, a0, a1, a2, g1, g2):
    n, d = h.shape
    row = pl.BlockSpec((ROW_TILE, d), lambda i: (i, 0))

    def whole(x):
        return pl.BlockSpec(x.shape, lambda i: (0,) * x.ndim)

    mu = mu.reshape(3, 1, d)
    w0 = w0.reshape(1, d)
    a0 = a0.reshape(1, d)
    consts = (mu, w0, w1, w2, a0, a1, a2, g1, g2)
    out = jax.ShapeDtypeStruct((n, d), F32)
    return pl.pallas_call(
        _lora_kernel,
        grid=(n // ROW_TILE,),
        in_specs=[row, row] + [whole(c) for c in consts],
        out_specs=[row, row, row],
        out_shape=[out, out, out],
        compiler_params=_params("parallel"),
        name="rwkv_lora",
    )(h, hp, *consts)


def _out_proj_kernel(op_ref, os_ref, w_ref, g_ref, x_ref, y_ref, *, prompt_tiles):
    o = jnp.where(pl.program_id(0) < prompt_tiles, op_ref[...], os_ref[...])
    y = _dot(o, w_ref[...])
    y_ref[...] = x_ref[...] + _rms(y) * g_ref[...]


def _out_proj(o_prompt, o_sample, w, g, xres):
    n, d = xres.shape
    k = o_prompt.shape[1]
    prompt_tiles = o_prompt.shape[0] // ROW_TILE
    return pl.pallas_call(
        functools.partial(_out_proj_kernel, prompt_tiles=prompt_tiles),
        grid=(n // ROW_TILE,),
        in_specs=_two_part_specs(k, prompt_tiles) + [
            pl.BlockSpec((k, d), lambda i: (0, 0)),
            pl.BlockSpec((1, d), lambda i: (0, 0)),
            pl.BlockSpec((ROW_TILE, d), lambda i: (i, 0))],
        out_specs=pl.BlockSpec((ROW_TILE, d), lambda i: (i, 0)),
        out_shape=jax.ShapeDtypeStruct((n, d), F32),
        compiler_params=_params("arbitrary"),
        name="out_proj",
    )(o_prompt, o_sample, w, g.reshape(1, d), xres)


def _t5_bucket(rel):
    nb = N_BUCKETS // 2
    max_exact = nb // 2
    offset = jnp.where(rel > 0, nb, 0)
    n = jnp.abs(rel)
    nf = jnp.maximum(n, 1).astype(F32)
    large = max_exact + (jnp.log(nf / max_exact) / math.log(MAX_DISTANCE / max_exact)
                         * (nb - max_exact)).astype(jnp.int32)
    large = jnp.minimum(large, nb - 1)
    return offset + jnp.where(n < max_exact, n, large)


def _bias_kernel(tt_ref, b_ref, o_ref):
    onehot = (lax.broadcasted_iota(jnp.int32, (tt_ref.shape[1], b_ref.shape[1]), 0) == b_ref[...]).astype(F32)
    o_ref[...] = jnp.dot(tt_ref[...], onehot, preferred_element_type=F32,
                         precision=lax.Precision.HIGHEST)


def _logit_offsets(table, sinks, n_q, n_k):
    assert n_k < KEY_PAD
    rel = (jnp.arange(n_k, dtype=jnp.int32)[None, :] - WINDOW - jnp.arange(n_q, dtype=jnp.int32)[:, None])
    rows = jnp.concatenate([_t5_bucket(rel).astype(jnp.int32),
                            jnp.full((n_q, 1), N_BUCKETS, jnp.int32),
                            jnp.full((n_q, KEY_PAD - n_k - 1), N_BUCKETS + 1, jnp.int32)], axis=1)
    n_heads = table.shape[1]
    values = jnp.concatenate([table.T, sinks.astype(F32)[:, None],
                              jnp.full((n_heads, 1), NEG_INF, F32)], axis=1)
    out = pl.pallas_call(
        _bias_kernel,
        out_shape=jax.ShapeDtypeStruct((n_heads, n_q * KEY_PAD), F32),
        name="logit_offsets",
    )(values, rows.reshape(1, n_q * KEY_PAD))
    return out.reshape(n_heads, n_q, KEY_PAD)


def _attn_kernel(*refs, n_parts, group, first_valid_step):
    q_ref = refs[0]
    k_refs = refs[1:1 + n_parts]
    v_refs = refs[1 + n_parts:1 + 2 * n_parts]
    off_ref, o_ref = refs[1 + 2 * n_parts:]
    n_q = q_ref.shape[0]
    step = pl.program_id(1)
    scale = HEAD_DIM ** -0.5
    part_rows = [r.shape[0] for r in k_refs]
    n_k = sum(part_rows)
    heads = range(N_KV_HEADS)
    tiles = range(group // 2)

    col = lax.broadcasted_iota(jnp.int32, (1, KEY_PAD), 1)
    hidden = None
    start = 0
    for p, rows in enumerate(part_rows):
        if first_valid_step[p] > 0:
            h_p = (col >= start) & (col < start + rows) & (step < first_valid_step[p])
            hidden = h_p if hidden is None else hidden | h_p
        start += rows

    zeros = jnp.zeros((KEY_PAD, HEAD_DIM), F32)
    ones = jnp.ones((KEY_PAD, HEAD_DIM), F32)

    def head_rows(refs_, hk):
        kv_cols = slice(hk * HEAD_DIM, (hk + 1) * HEAD_DIM)
        return jnp.concatenate([r[:, kv_cols] for r in refs_] + [zeros[:KEY_PAD - n_k]], axis=0)

    scores = []
    for hk in heads:
        k = head_rows(k_refs, hk)
        k2 = jnp.concatenate([jnp.concatenate([k, zeros], axis=1),
                              jnp.concatenate([zeros, k], axis=1)], axis=0).astype(BF16)
        q_rows = jnp.concatenate(
            [q_ref[:, (hk * group + 2 * t) * HEAD_DIM:(hk * group + 2 * t + 2) * HEAD_DIM] for t in tiles],
            axis=0)
        off = jnp.concatenate(
            [jnp.concatenate([off_ref[hk * group + 2 * t], off_ref[hk * group + 2 * t + 1]], axis=1)
             for t in tiles], axis=0)
        s = _dot_nt(q_rows, k2) * scale + off
        if hidden is not None:
            s = jnp.where(jnp.concatenate([hidden, hidden], axis=1), NEG_INF, s)
        scores.append(s)

    expo = []
    for hk in heads:
        s = scores[hk]
        m0 = jnp.max(s[:, :KEY_PAD], axis=-1, keepdims=True)
        m1 = jnp.max(s[:, KEY_PAD:], axis=-1, keepdims=True)
        expo.append(jnp.concatenate([jnp.exp(s[:, :KEY_PAD] - m0), jnp.exp(s[:, KEY_PAD:] - m1)],
                                    axis=1).astype(BF16))

    for hk in heads:
        v = head_rows(v_refs, hk)
        rhs = jnp.concatenate([jnp.concatenate([v, zeros, ones, zeros], axis=1),
                               jnp.concatenate([zeros, v, zeros, ones], axis=1)], axis=0).astype(BF16)
        acc = _dot(expo[hk], rhs)
        o = acc[:, :LANES] / acc[:, LANES:]
        for t in tiles:
            first_head = hk * group + 2 * t
            o_ref[:, first_head * HEAD_DIM:(first_head + 2) * HEAD_DIM] = (
                o[t * n_q:(t + 1) * n_q].astype(o_ref.dtype))


def _attention(q, parts, offsets, *, grid, q_rows, q_map, n_out_rows):
    q_dim = q.shape[1]
    n_heads = q_dim // HEAD_DIM
    group = n_heads // N_KV_HEADS
    assert offsets.shape == (n_heads, q_rows, KEY_PAD) and group % 2 == 0

    def kv_spec(arr, rows, imap):
        if arr.ndim == 3:
            return pl.BlockSpec((None, rows, arr.shape[2]), imap)
        return pl.BlockSpec((rows, arr.shape[1]), imap)

    k_specs = [kv_spec(p[0], p[2], p[3]) for p in parts]
    v_specs = [kv_spec(p[1], p[2], p[3]) for p in parts]
    n_steps = grid[1]
    return pl.pallas_call(
        functools.partial(_attn_kernel, n_parts=len(parts), group=group,
                          first_valid_step=tuple(p[4] for p in parts)),
        grid=grid,
        in_specs=[pl.BlockSpec((q_rows, q_dim), q_map)] + k_specs + v_specs + [
            pl.BlockSpec(offsets.shape, lambda b, c: (0, 0, 0))],
        out_specs=pl.BlockSpec((q_rows, q_dim), lambda b, c: (b * n_steps + c, 0)),
        out_shape=jax.ShapeDtypeStruct((n_out_rows, q_dim), BF16),
        compiler_params=_params("parallel", "arbitrary"),
        name="swa_attention",
    )(q, *[p[0] for p in parts], *[p[1] for p in parts], offsets)


def _split3(x):
    hi = x.astype(BF16)
    r1 = x - hi.astype(F32)
    mid = r1.astype(BF16)
    lo = (r1 - mid.astype(F32)).astype(BF16)
    return hi, mid, lo


def _wkv_kernel(r_ref, k_ref, v_ref, lw_ref, al_ref, gate_ref, kk_ref, ka_ref, rk_ref, lnw_ref,
                lnb_ref, s0_ref, o_ref, s_out_ref, h_ref):
    L = WKV_CHUNK
    L2 = 2 * L
    valid = r_ref.shape[0]
    n_pairs = r_ref.shape[1] // LANES
    c_idx = pl.program_id(1)
    half = RWKV_HEAD

    lane = lax.broadcasted_iota(jnp.int32, (1, LANES), 1)
    head0 = lane < half
    row2 = lax.broadcasted_iota(jnp.int32, (L2, L2), 0)
    col2 = lax.broadcasted_iota(jnp.int32, (L2, L2), 1)
    strict_lower = row2 > col2
    lower = row2 >= col2
    eye2 = (row2 == col2).astype(F32)
    tril = (lax.broadcasted_iota(jnp.int32, (L, L), 0) >= lax.broadcasted_iota(jnp.int32, (L, L), 1)
            ).astype(BF16)
    krow = lax.broadcasted_iota(jnp.int32, (LANES, LANES), 0)
    kcol = lax.broadcasted_iota(jnp.int32, (LANES, LANES), 1)
    same_head = (krow < half) == (kcol < half)
    eye_k = (krow == kcol).astype(F32)

    def load(ref, cs):
        x = ref[:, cs]
        if valid < L:
            x = jnp.concatenate([x, jnp.zeros((L - valid, LANES), x.dtype)], axis=0)
        return x

    def seg_sum(x):
        s0 = jnp.sum(jnp.where(head0, x, 0.0), axis=-1, keepdims=True)
        s1 = jnp.sum(jnp.where(head0, 0.0, x), axis=-1, keepdims=True)
        return jnp.where(head0, s0, s1)

    def stack2(x):
        return jnp.concatenate([jnp.where(head0, x, 0.0), jnp.where(head0, 0.0, x)], axis=0)

    def fold(m):
        return m[:L] + m[L:]

    @pl.when(c_idx == 0)
    def _():
        for p in range(n_pairs):
            s0 = jnp.concatenate([s0_ref[2 * p].T, s0_ref[2 * p + 1].T], axis=0)
            h_ref[p] = jnp.where(same_head, jnp.concatenate([s0, s0], axis=1), 0.0)

    pairs = range(n_pairs)
    cols = [slice(p * LANES, (p + 1) * LANES) for p in pairs]

    r = [load(r_ref, cs) for cs in cols]
    v = [load(v_ref, cs) for cs in cols]
    lw = [load(lw_ref, cs) for cs in cols]
    c3 = [_dot(tril, jnp.concatenate(_split3(x), axis=1)) for x in lw]
    c = [x[:, :LANES] + x[:, LANES:2 * LANES] + x[:, 2 * LANES:] for x in c3]

    k, a, b = [], [], []
    for p in pairs:
        k_raw = load(k_ref, cols[p])
        al = load(al_ref, cols[p])
        kk = k_raw * kk_ref[:, cols[p]]
        kk = kk / jnp.maximum(jnp.sqrt(seg_sum(kk * kk)), 1e-12)
        k.append(k_raw * (1.0 + (al - 1.0) * ka_ref[:, cols[p]]))
        a.append(-kk)
        b.append(kk * al)

    at2, v2, rt, bc2, kc2, d_end, gram = [], [], [], [], [], [], []
    for p in pairs:
        c_last = c[p][L - 1:L]
        e_neg = jnp.exp(-c[p])
        e_end = jnp.exp(c_last - c[p])
        at2.append(stack2(a[p] * jnp.exp(c[p] - lw[p])))
        rt.append(r[p] * jnp.exp(c[p]))
        v2.append(stack2(v[p]))
        bc2.append(stack2(b[p] * e_end))
        kc2.append(stack2(k[p] * e_end))
        d_end.append(jnp.exp(c_last))
        gram.append(_dot_nt(
            jnp.concatenate([at2[p], stack2(rt[p])], axis=0).astype(BF16),
            jnp.concatenate([stack2(b[p] * e_neg), stack2(k[p] * e_neg)], axis=0).astype(BF16)))

    n_mat = [jnp.where(strict_lower, g[:L2, :L2], 0.0) for g in gram]
    akv2 = [_dot(jnp.where(strict_lower, gram[p][:L2, L2:], 0.0).astype(BF16), v2[p].astype(BF16))
            for p in pairs]
    lhs_y = [jnp.concatenate([fold(jnp.where(lower, g[L2:, :L2], 0.0)),
                              fold(jnp.where(lower, g[L2:, L2:], 0.0))], axis=1) for g in gram]

    size = 1
    t_inv = [eye2 for _ in pairs]
    while size < L:
        quadrant = ((row2 // (2 * size)) == (col2 // (2 * size))) & \
                   ((row2 // size) % 2 == 1) & ((col2 // size) % 2 == 0)
        n_c = [jnp.where(quadrant, n, 0.0) for n in n_mat]
        if size == 1:
            t_inv = [t + n for t, n in zip(t_inv, n_c)]
        else:
            t_bf = [t.astype(BF16) for t in t_inv]
            tn = [_dot(t, n.astype(BF16)).astype(BF16) for t, n in zip(t_bf, n_c)]
            t_inv = [t + _dot(x, tb) for t, x, tb in zip(t_inv, tn, t_bf)]
        size *= 2

    u2 = [_dot(t_inv[p].astype(BF16), jnp.concatenate([at2[p], akv2[p]], axis=1).astype(BF16))
          for p in pairs]
    yz = []
    for p in pairs:
        z = jnp.concatenate(
            [u2[p], jnp.concatenate([jnp.zeros((L2, LANES), F32), v2[p]], axis=1)], axis=0).astype(BF16)
        lhs_h = jnp.concatenate([bc2[p], kc2[p]], axis=0).T
        yz.append(_dot(jnp.concatenate([lhs_y[p], lhs_h], axis=0).astype(BF16), z))

    hh = []
    for p in pairs:
        q_hat = rt[p] + yz[p][:L, :LANES]
        m_mat = eye_k * d_end[p] + yz[p][L:, :LANES]
        hh.append(_dot(jnp.concatenate([m_mat, q_hat], axis=0).astype(BF16), h_ref[p].astype(BF16)))

    for p in pairs:
        h_ref[p] = jnp.where(same_head, hh[p][:LANES] + yz[p][L:, LANES:], 0.0)
        y = hh[p][LANES:] + yz[p][:L, LANES:]
        mean = seg_sum(y) * (1.0 / half)
        yc = y - mean
        var = seg_sum(yc * yc) * (1.0 / half)
        yn = yc * lax.rsqrt(var + GN_EPS) * lnw_ref[:, cols[p]] + lnb_ref[:, cols[p]]
        bonus = seg_sum(r[p] * k[p] * rk_ref[:, cols[p]]) * v[p]
        out = yn + bonus
        o_ref[:, cols[p]] = (out[:valid] * gate_ref[:, cols[p]]).astype(o_ref.dtype)

    @pl.when(c_idx == pl.num_programs(1) - 1)
    def _():
        for p in range(n_pairs):
            h_last = h_ref[p]
            s_out_ref[2 * p] = h_last[:half, :half].T
            s_out_ref[2 * p + 1] = h_last[half:, half:].T


def _wkv(rkv, lw, al, gate, k_k, k_a, r_k, ln_w, ln_b, state, *, n_streams, n_chunks, rows, first_block):
    d = lw.shape[1]
    n_pairs = d // LANES

    def block(b, c):
        return first_block + b * n_chunks + c

    seq = pl.BlockSpec((rows, d), lambda b, c: (block(b, c), 0))
    rkv_specs = [pl.BlockSpec((None, rows, d), lambda b, c, s=s: (s, block(b, c), 0)) for s in range(3)]
    vec = pl.BlockSpec((1, d), lambda b, c: (0, 0))
    st = pl.BlockSpec((None,) + state.shape[1:], lambda b, c: (b, 0, 0, 0))
    return pl.pallas_call(
        _wkv_kernel,
        grid=(n_streams, n_chunks),
        in_specs=rkv_specs + [seq] * 3 + [vec] * 5 + [st],
        out_specs=[pl.BlockSpec((rows, d), lambda b, c: (b * n_chunks + c, 0)), st],
        out_shape=[jax.ShapeDtypeStruct((n_streams * n_chunks * rows, d), BF16),
                   jax.ShapeDtypeStruct(state.shape, F32)],
        scratch_shapes=[pltpu.VMEM((n_pairs, LANES, LANES), F32)],
        compiler_params=_params("parallel", "arbitrary"),
        name="rwkv7_wkv",
    )(rkv, rkv, rkv, lw, al, gate, *[x.reshape(1, d) for x in (k_k, k_a, r_k, ln_w, ln_b)], state)


def _pad_lora(w_in, w_out):
    rank = w_in.shape[1]
    pad = (-rank) % LORA_PAD
    return (jnp.pad(w_in, ((0, 0), (0, pad))).astype(BF16),
            jnp.pad(w_out, ((0, pad), (0, 0))).astype(BF16))


def kernel(x_prompt, x_sample, cache_k, cache_v, state_shift, state_wkv, norm_g, ffn_w_gate, ffn_w_up, ffn_w_down, rel_table, att_w_qkv, att_b_qkv, att_w_o, att_sinks, rwkv_mu, rwkv_w_r, rwkv_w_k, rwkv_w_v, rwkv_w_o, rwkv_w0, rwkv_w1, rwkv_w2, rwkv_a0, rwkv_a1, rwkv_a2, rwkv_g1, rwkv_g2, rwkv_k_k, rwkv_k_a, rwkv_r_k, rwkv_ln_w, rwkv_ln_b):
    n_bp, t_p, d = x_prompt.shape
    n_bs, t_s, _ = x_sample.shape
    n_p = n_bp * t_p
    n_s = n_bs * t_s
    depth = norm_g.shape[0]
    q_dim = att_w_o.shape[1]
    kv_dim = N_KV_HEADS * HEAD_DIM
    assert n_p % ROW_TILE == 0 and n_s % ROW_TILE == 0 and ROW_TILE % t_s == 0 and t_p % CHUNK == 0
    assert t_p % WKV_CHUNK == 0 and t_s <= WKV_CHUNK and n_p % t_s == 0

    ffn_w = (ffn_w_gate.astype(BF16), ffn_w_up.astype(BF16), ffn_w_down.astype(BF16))

    x = (x_prompt.reshape(n_p, d), x_sample.reshape(n_s, d))
    new_k, new_v, new_shift, new_wkv = [], [], [], []
    for i in range(depth):
        g = norm_g[i]
        j = i // 2
        x = _ffn_block(x, g[0], g[1], ffn_w, i, 0)
        if i % 2 == 0:
            q, k, v = _qkv_proj(x, g[2], att_w_qkv[j].astype(BF16), att_b_qkv[j], q_dim, kv_dim)

            n_back = WINDOW // CHUNK
            n_c = t_p // CHUNK
            parts = [(k, v, CHUNK, (lambda b, c, back=back: (b * n_c + jnp.maximum(c - back, 0), 0)), back)
                     for back in range(n_back, -1, -1)]
            o_p = _attention(q, parts, _logit_offsets(rel_table, att_sinks[j], CHUNK, (n_back + 1) * CHUNK),
                             grid=(n_bp, n_c), q_rows=CHUNK, q_map=lambda b, c: (b * n_c + c, 0),
                             n_out_rows=n_p)
            first = n_p // t_s
            parts = [(cache_k[j].reshape(n_bs, WINDOW, kv_dim), cache_v[j].reshape(n_bs, WINDOW, kv_dim),
                      WINDOW, (lambda b, c: (b, 0, 0)), 0),
                     (k, v, t_s, (lambda b, c: (first + b, 0)), 0)]
            o_s = _attention(q, parts, _logit_offsets(rel_table, att_sinks[j], t_s, WINDOW + t_s),
                             grid=(n_bs, 1), q_rows=t_s, q_map=lambda b, c: (first + b, 0),
                             n_out_rows=n_s)
            x = _out_proj(o_p, o_s, att_w_o[j].astype(BF16), g[3], x)

            def kv_rows(a):
                newest = jnp.stack([a[(b + 1) * t_p - WINDOW:(b + 1) * t_p] for b in range(n_bp)])
                return (newest.reshape(n_bp, WINDOW, N_KV_HEADS, HEAD_DIM),
                        a[n_p:].reshape(n_bs, t_s, N_KV_HEADS, HEAD_DIM))

            new_k.append(kv_rows(k))
            new_v.append(kv_rows(v))
        else:
            first_rows = jnp.broadcast_to(state_shift[j], (n_bs, t_s, d)).reshape(n_s, d)
            h, hp = _norm_shift(x, g[2], first_rows, n_p, t_p, t_s)
            mu = rwkv_mu[j]
            rkv = _rkv_proj(h, hp, jnp.stack([mu[0], mu[2], mu[3]]),
                            jnp.stack([rwkv_w_r[j], rwkv_w_k[j], rwkv_w_v[j]]).astype(BF16))
            w1, w2 = _pad_lora(rwkv_w1[j], rwkv_w2[j])
            a1, a2 = _pad_lora(rwkv_a1[j], rwkv_a2[j])
            g1, g2 = _pad_lora(rwkv_g1[j], rwkv_g2[j])
            lw, al, gate = _lora_branches(h, hp, jnp.stack([mu[1], mu[4], mu[5]]), rwkv_w0[j], w1, w2,
                                          rwkv_a0[j], a1, a2, g1, g2)
            vecs = (rwkv_k_k[j], rwkv_k_a[j], rwkv_r_k[j].reshape(d), rwkv_ln_w[j], rwkv_ln_b[j])
            zero_state = jnp.zeros((n_bp, d // RWKV_HEAD, RWKV_HEAD, RWKV_HEAD), F32)
            o_p, wkv_p = _wkv(rkv, lw, al, gate, *vecs, zero_state, n_streams=n_bp,
                              n_chunks=t_p // WKV_CHUNK, rows=WKV_CHUNK, first_block=0)
            o_s, wkv_s = _wkv(rkv, lw, al, gate, *vecs, state_wkv[j], n_streams=n_bs,
                              n_chunks=1, rows=t_s, first_block=n_p // t_s)
            x = _out_proj(o_p, o_s, rwkv_w_o[j].astype(BF16), g[3], x)
            new_shift.append((h[t_p - 1:n_p:t_p].reshape(n_bp, 1, d),
                              h[n_p + t_s - 1::t_s].reshape(n_bs, 1, d)))
            new_wkv.append((wkv_p, wkv_s))
        x = _ffn_block(x, g[4], g[5], ffn_w, i, 1, split_rows=n_p if i == depth - 1 else None)

    def both(pairs):
        return jnp.stack([p[0] for p in pairs]), jnp.stack([p[1] for p in pairs])

    k_prompt, k_sample = both(new_k)
    v_prompt, v_sample = both(new_v)
    shift_prompt, shift_sample = both(new_shift)
    wkv_prompt, wkv_sample = both(new_wkv)
    return (x[0].reshape(n_bp, t_p, d), x[1].reshape(n_bs, t_s, d),
            k_prompt, v_prompt, k_sample, v_sample,
            shift_prompt, wkv_prompt, shift_sample, wkv_sample)
```

```python
import functools
import math

import jax
import jax.numpy as jnp
from jax import lax
from jax.experimental import pallas as pl
from jax.experimental.pallas import tpu as pltpu

F32 = jnp.float32
BF16 = jnp.bfloat16

HEAD_DIM = 64
N_KV_HEADS = 4
CHUNK = 64
WINDOW = 128
N_BUCKETS = 32
MAX_DISTANCE = 128
RWKV_HEAD = 64
RMS_EPS = 1e-6
GN_EPS = RWKV_HEAD * 1e-5
NEG_INF = -1e30
LORA_PAD = 128

LANES = 128
KEY_PAD = 256
VMEM_LIMIT_BYTES = 56 * 1024 * 1024
ROW_TILE = 512
RWKV_ROW_TILE = 256
FF_TILE = 512
WKV_CHUNK = 64


def _params(*semantics):
    return pltpu.CompilerParams(dimension_semantics=semantics,
                                vmem_limit_bytes=VMEM_LIMIT_BYTES)


def _rms(x):
    return x * lax.rsqrt(jnp.mean(jnp.square(x), axis=-1, keepdims=True) + RMS_EPS)


def _dot(a, b):
    return jnp.dot(a, b, preferred_element_type=F32)


def _dot_nt(a, b):
    return lax.dot_general(a, b, (((1,), (1,)), ((), ())), preferred_element_type=F32)


def _two_part_specs(d, prompt_tiles):
    return [pl.BlockSpec((ROW_TILE, d), lambda i, *_: (jnp.minimum(i, prompt_tiles - 1), 0)),
            pl.BlockSpec((ROW_TILE, d), lambda i, *_: (jnp.maximum(i - prompt_tiles, 0), 0))]


def _ffn_kernel(*refs, n_x, n_out, prompt_tiles):
    x_refs = refs[:n_x]
    g0_ref, g1_ref, wg_ref, wu_ref, wd_ref = refs[n_x:n_x + 5]
    o_refs = refs[n_x + 5:n_x + 5 + n_out]
    h_ref = refs[n_x + 5 + n_out]
    i = pl.program_id(0)
    f = pl.program_id(1)

    def partial_down(h):
        a = _dot(h, wg_ref[...])
        b = _dot(h, wu_ref[...])
        return _dot((a * jax.nn.sigmoid(a) * b).astype(BF16), wd_ref[...])

    def body(x_ref, o_ref):
        last = pl.num_programs(1) - 1

        @pl.when(f == 0)
        def _():
            h = (_rms(x_ref[...]) * g0_ref[...]).astype(BF16)
            h_ref[...] = h
            o_ref[...] = partial_down(h)

        @pl.when((f > 0) & (f < last))
        def _():
            o_ref[...] += partial_down(h_ref[...])

        @pl.when(f == last)
        def _():
            y = o_ref[...] + partial_down(h_ref[...])
            o_ref[...] = x_ref[...] + 0.5 * (_rms(y) * g1_ref[...])

    if n_x == 1 and n_out == 1:
        body(x_refs[0], o_refs[0])
    else:
        pl.when(i < prompt_tiles)(lambda: body(x_refs[0], o_refs[0]))
        pl.when(i >= prompt_tiles)(lambda: body(x_refs[-1], o_refs[-1]))


def _ffn_block(x, g0, g1, w, layer, half, split_rows=None):
    xs = x if isinstance(x, tuple) else (x,)
    d = xs[0].shape[1]
    n = sum(a.shape[0] for a in xs)
    d_ff = w[0].shape[-1]
    assert d_ff // FF_TILE >= 2
    row = pl.BlockSpec((ROW_TILE, d), lambda i, f: (i, 0))
    prompt_tiles = n // ROW_TILE
    x_specs, out_specs, out_shape = [row], [row], [jax.ShapeDtypeStruct((n, d), F32)]
    if len(xs) == 2:
        prompt_tiles = xs[0].shape[0] // ROW_TILE
        x_specs = _two_part_specs(d, prompt_tiles)
    if split_rows is not None:
        prompt_tiles = split_rows // ROW_TILE
        out_specs = _two_part_specs(d, prompt_tiles)
        out_shape = [jax.ShapeDtypeStruct((split_rows, d), F32),
                     jax.ShapeDtypeStruct((n - split_rows, d), F32)]
    out = pl.pallas_call(
        functools.partial(_ffn_kernel, n_x=len(xs), n_out=len(out_shape), prompt_tiles=prompt_tiles),
        grid=(n // ROW_TILE, d_ff // FF_TILE),
        in_specs=x_specs + [
            pl.BlockSpec((1, d), lambda i, f: (0, 0)),
            pl.BlockSpec((1, d), lambda i, f: (0, 0)),
            pl.BlockSpec((None, None, d, FF_TILE), lambda i, f: (layer, half, 0, f)),
            pl.BlockSpec((None, None, d, FF_TILE), lambda i, f: (layer, half, 0, f)),
            pl.BlockSpec((None, None, FF_TILE, d), lambda i, f: (layer, half, f, 0)),
        ],
        out_specs=out_specs,
        out_shape=out_shape,
        scratch_shapes=[pltpu.VMEM((ROW_TILE, d), BF16)],
        compiler_params=_params("arbitrary", "arbitrary"),
        name="ffn_block",
    )(*xs, g0.reshape(1, d), g1.reshape(1, d), *w)
    return tuple(out) if split_rows is not None else out[0]


def _normed_and_shifted(x_ref, g_ref, first_ref, carry_ref, streams):
    prompt_tiles, prompt_len, sample_len = streams
    tile = x_ref.shape[0]
    i = pl.program_id(0)

    @pl.when(i == 0)
    def _():
        carry_ref[...] = jnp.zeros_like(carry_ref)

    h = _rms(x_ref[...]) * g_ref[...]
    rows = lax.broadcasted_iota(jnp.int32, (tile, 1), 0)
    prev = jnp.where(rows == 0, carry_ref[...], pltpu.roll(h, 1, axis=0))
    carry_ref[...] = h[tile - 1:tile]
    prompt_start = (rows + i * tile) % prompt_len == 0
    sample_start = rows % sample_len == 0
    return h, jnp.where(i < prompt_tiles, jnp.where(prompt_start, 0.0, prev),
                        jnp.where(sample_start, first_ref[...], prev))


def _shift_specs(d, tile, n_prompt_rows):
    prompt_tiles = n_prompt_rows // tile
    return [pl.BlockSpec((tile, d), lambda i: (i, 0)),
            pl.BlockSpec((1, d), lambda i: (0, 0)),
            pl.BlockSpec((tile, d), lambda i: (jnp.maximum(i - prompt_tiles, 0), 0))]


def _qkv_kernel(x_ref, g_ref, w_ref, b_ref, q_ref, k_ref, v_ref):
    h = (_rms(x_ref[...]) * g_ref[...]).astype(BF16)
    y = _dot(h, w_ref[...]) + b_ref[...]
    q_dim = q_ref.shape[1]
    kv_dim = k_ref.shape[1]
    q_ref[...] = y[:, :q_dim].astype(q_ref.dtype)
    k_ref[...] = y[:, q_dim:q_dim + kv_dim]
    v_ref[...] = y[:, q_dim + kv_dim:]


def _qkv_proj(x, g, w, b, q_dim, kv_dim):
    n, d = x.shape
    m = w.shape[1]

    def rows(width):
        return pl.BlockSpec((ROW_TILE, width), lambda i: (i, 0))

    return pl.pallas_call(
        _qkv_kernel,
        grid=(n // ROW_TILE,),
        in_specs=[rows(d), pl.BlockSpec((1, d), lambda i: (0, 0)),
                  pl.BlockSpec((d, m), lambda i: (0, 0)), pl.BlockSpec((1, m), lambda i: (0, 0))],
        out_specs=[rows(q_dim), rows(kv_dim), rows(kv_dim)],
        out_shape=[jax.ShapeDtypeStruct((n, q_dim), BF16), jax.ShapeDtypeStruct((n, kv_dim), F32),
                   jax.ShapeDtypeStruct((n, kv_dim), F32)],
        compiler_params=_params("parallel"),
        name="qkv_proj",
    )(x, g.reshape(1, d), w, b.reshape(1, m))


def _rkv_kernel(x_ref, g_ref, first_ref, mu_ref, w_ref, o_ref, carry_ref, *, streams):
    h, hp = _normed_and_shifted(x_ref, g_ref, first_ref, carry_ref, streams)
    dx = hp - h
    for s in range(w_ref.shape[0]):
        o_ref[s] = _dot((h + dx * mu_ref[s]).astype(BF16), w_ref[s])


def _rkv_proj(x, g, first_rows, streams, mu, w):
    n, d = x.shape
    k = w.shape[0]
    m = w.shape[2]
    n_prompt_rows, prompt_len, sample_len = streams
    return pl.pallas_call(
        functools.partial(_rkv_kernel, streams=(n_prompt_rows // RWKV_ROW_TILE, prompt_len, sample_len)),
        grid=(n // RWKV_ROW_TILE,),
        in_specs=_shift_specs(d, RWKV_ROW_TILE, n_prompt_rows) + [
            pl.BlockSpec((k, 1, d), lambda i: (0, 0, 0)),
            pl.BlockSpec((k, d, m), lambda i: (0, 0, 0), pipeline_mode=pl.Buffered(1))],
        out_specs=pl.BlockSpec((k, RWKV_ROW_TILE, m), lambda i: (0, i, 0)),
        out_shape=jax.ShapeDtypeStruct((k, n, m), F32),
        scratch_shapes=[pltpu.VMEM((1, d), F32)],
        compiler_params=_params("arbitrary"),
        name="rkv_proj",
    )(x, g.reshape(1, d), first_rows, mu.reshape(k, 1, d), w)


def _lora_kernel(x_ref, g_ref, first_ref, mu_ref, w0_ref, w1_ref, w2_ref, a0_ref, a1_ref, a2_ref,
                 g1_ref, g2_ref, lw_ref, al_ref, gate_ref, tail_ref, carry_ref, *, streams):
    h, hp = _normed_and_shifted(x_ref, g_ref, first_ref, carry_ref, streams)
    dx = hp - h
    sample_len = streams[2]
    tail_ref[...] = jnp.concatenate(
        [h[(t + 1) * sample_len - 1:(t + 1) * sample_len] for t in range(tail_ref.shape[0])], axis=0)

    xw = (h + dx * mu_ref[0]).astype(BF16)
    t = jnp.tanh(_dot(xw, w1_ref[...])).astype(BF16)
    z = -(w0_ref[...] + _dot(t, w2_ref[...]))
    softplus = jnp.maximum(z, 0.0) + jnp.log(1.0 + jnp.exp(-jnp.abs(z)))
    lw_ref[...] = -jnp.exp(-softplus - 0.5)

    xa = (h + dx * mu_ref[1]).astype(BF16)
    u = _dot(xa, a1_ref[...]).astype(BF16)
    al_ref[...] = jax.nn.sigmoid(a0_ref[...] + _dot(u, a2_ref[...]))

    xg = (h + dx * mu_ref[2]).astype(BF16)
    s = jax.nn.sigmoid(_dot(xg, g1_ref[...])).astype(BF16)
    gate_ref[...] = _dot(s, g2_ref[...])


def _lora_branches(x, g, first_rows, streams, mu, w0, w1, w2, a0, a1, a2, g1, g2):
    n, d = x.shape
    n_prompt_rows, prompt_len, sample_len = streams
    tile = RWKV_ROW_TILE
    row = pl.BlockSpec((tile, d), lambda i: (i, 0))
    tail = pl.BlockSpec((tile // sample_len, d), lambda i: (i, 0))

    def whole(a):
        return pl.BlockSpec(a.shape, lambda i: (0,) * a.ndim)

    mu = mu.reshape(3, 1, d)
    w0 = w0.reshape(1, d)
    a0 = a0.reshape(1, d)
    consts = (mu, w0, w1, w2, a0, a1, a2, g1, g2)
    out = jax.ShapeDtypeStruct((n, d), F32)
    return pl.pallas_call(
        functools.partial(_lora_kernel, streams=(n_prompt_rows // tile, prompt_len, sample_len)),
        grid=(n // tile,),
        in_specs=_shift_specs(d, tile, n_prompt_rows) + [whole(c) for c in consts],
        out_specs=[row, row, row, tail],
        out_shape=[out, out, out, jax.ShapeDtypeStruct((n // sample_len, d), F32)],
        scratch_shapes=[pltpu.VMEM((1, d), F32)],
        compiler_params=_params("arbitrary"),
        name="rwkv_lora",
    )(x, g.reshape(1, d), first_rows, *consts)


def _out_proj_kernel(op_ref, os_ref, w_ref, g_ref, x_ref, y_ref, *, prompt_tiles):
    o = jnp.where(pl.program_id(0) < prompt_tiles, op_ref[...], os_ref[...])
    y = _dot(o, w_ref[...])
    y_ref[...] = x_ref[...] + _rms(y) * g_ref[...]


def _out_proj(o_prompt, o_sample, w, g, xres):
    n, d = xres.shape
    k = o_prompt.shape[1]
    prompt_tiles = o_prompt.shape[0] // ROW_TILE
    return pl.pallas_call(
        functools.partial(_out_proj_kernel, prompt_tiles=prompt_tiles),
        grid=(n // ROW_TILE,),
        in_specs=_two_part_specs(k, prompt_tiles) + [
            pl.BlockSpec((k, d), lambda i: (0, 0)),
            pl.BlockSpec((1, d), lambda i: (0, 0)),
            pl.BlockSpec((ROW_TILE, d), lambda i: (i, 0))],
        out_specs=pl.BlockSpec((ROW_TILE, d), lambda i: (i, 0)),
        out_shape=jax.ShapeDtypeStruct((n, d), F32),
        compiler_params=_params("arbitrary"),
        name="out_proj",
    )(o_prompt, o_sample, w, g.reshape(1, d), xres)


def _t5_bucket(rel):
    nb = N_BUCKETS // 2
    max_exact = nb // 2
    offset = jnp.where(rel > 0, nb, 0)
    n = jnp.abs(rel)
    nf = jnp.maximum(n, 1).astype(F32)
    large = max_exact + (jnp.log(nf / max_exact) / math.log(MAX_DISTANCE / max_exact)
                         * (nb - max_exact)).astype(jnp.int32)
    large = jnp.minimum(large, nb - 1)
    return offset + jnp.where(n < max_exact, n, large)


def _bias_kernel(tt_ref, b_ref, o_ref):
    onehot = (lax.broadcasted_iota(jnp.int32, (tt_ref.shape[1], b_ref.shape[1]), 0) == b_ref[...]).astype(F32)
    o_ref[...] = jnp.dot(tt_ref[...], onehot, preferred_element_type=F32,
                         precision=lax.Precision.HIGHEST)


def _logit_offsets(table, sinks, n_q, n_k):
    assert n_k < KEY_PAD
    rel = (jnp.arange(n_k, dtype=jnp.int32)[None, :] - WINDOW - jnp.arange(n_q, dtype=jnp.int32)[:, None])
    rows = jnp.concatenate([_t5_bucket(rel).astype(jnp.int32),
                            jnp.full((n_q, 1), N_BUCKETS, jnp.int32),
                            jnp.full((n_q, KEY_PAD - n_k - 1), N_BUCKETS + 1, jnp.int32)], axis=1)
    n_heads = table.shape[1]
    values = jnp.concatenate([table.T, sinks.astype(F32)[:, None],
                              jnp.full((n_heads, 1), NEG_INF, F32)], axis=1)
    out = pl.pallas_call(
        _bias_kernel,
        out_shape=jax.ShapeDtypeStruct((n_heads, n_q * KEY_PAD), F32),
        name="logit_offsets",
    )(values, rows.reshape(1, n_q * KEY_PAD))
    return out.reshape(n_heads, n_q, KEY_PAD)


def _attn_kernel(*refs, n_parts, group, first_valid_step):
    q_ref = refs[0]
    k_refs = refs[1:1 + n_parts]
    v_refs = refs[1 + n_parts:1 + 2 * n_parts]
    off_ref, o_ref = refs[1 + 2 * n_parts:]
    n_q = q_ref.shape[0]
    step = pl.program_id(1)
    scale = HEAD_DIM ** -0.5
    part_rows = [r.shape[0] for r in k_refs]
    n_k = sum(part_rows)
    heads = range(N_KV_HEADS)
    tiles = range(group // 2)

    col = lax.broadcasted_iota(jnp.int32, (1, KEY_PAD), 1)
    hidden = None
    start = 0
    for p, rows in enumerate(part_rows):
        if first_valid_step[p] > 0:
            h_p = (col >= start) & (col < start + rows) & (step < first_valid_step[p])
            hidden = h_p if hidden is None else hidden | h_p
        start += rows

    zeros = jnp.zeros((KEY_PAD, HEAD_DIM), F32)
    ones = jnp.ones((KEY_PAD, HEAD_DIM), F32)

    def head_rows(refs_, hk):
        kv_cols = slice(hk * HEAD_DIM, (hk + 1) * HEAD_DIM)
        return jnp.concatenate([r[:, kv_cols] for r in refs_] + [zeros[:KEY_PAD - n_k]], axis=0)

    scores = []
    for hk in heads:
        k = head_rows(k_refs, hk)
        k2 = jnp.concatenate([jnp.concatenate([k, zeros], axis=1),
                              jnp.concatenate([zeros, k], axis=1)], axis=0).astype(BF16)
        q_rows = jnp.concatenate(
            [q_ref[:, (hk * group + 2 * t) * HEAD_DIM:(hk * group + 2 * t + 2) * HEAD_DIM] for t in tiles],
            axis=0)
        off = jnp.concatenate(
            [jnp.concatenate([off_ref[hk * group + 2 * t], off_ref[hk * group + 2 * t + 1]], axis=1)
             for t in tiles], axis=0)
        s = _dot_nt(q_rows, k2) * scale + off
        if hidden is not None:
            s = jnp.where(jnp.concatenate([hidden, hidden], axis=1), NEG_INF, s)
        scores.append(s)

    expo = []
    for hk in heads:
        s = scores[hk]
        m0 = jnp.max(s[:, :KEY_PAD], axis=-1, keepdims=True)
        m1 = jnp.max(s[:, KEY_PAD:], axis=-1, keepdims=True)
        expo.append(jnp.concatenate([jnp.exp(s[:, :KEY_PAD] - m0), jnp.exp(s[:, KEY_PAD:] - m1)],
                                    axis=1).astype(BF16))

    for hk in heads:
        v = head_rows(v_refs, hk)
        rhs = jnp.concatenate([jnp.concatenate([v, zeros, ones, zeros], axis=1),
                               jnp.concatenate([zeros, v, zeros, ones], axis=1)], axis=0).astype(BF16)
        acc = _dot(expo[hk], rhs)
        o = acc[:, :LANES] / acc[:, LANES:]
        for t in tiles:
            first_head = hk * group + 2 * t
            o_ref[:, first_head * HEAD_DIM:(first_head + 2) * HEAD_DIM] = (
                o[t * n_q:(t + 1) * n_q].astype(o_ref.dtype))


def _attention(q, parts, offsets, *, grid, q_rows, q_map, n_out_rows):
    q_dim = q.shape[1]
    n_heads = q_dim // HEAD_DIM
    group = n_heads // N_KV_HEADS
    assert offsets.shape == (n_heads, q_rows, KEY_PAD) and group % 2 == 0

    def kv_spec(arr, rows, imap):
        if arr.ndim == 3:
            return pl.BlockSpec((None, rows, arr.shape[2]), imap)
        return pl.BlockSpec((rows, arr.shape[1]), imap)

    k_specs = [kv_spec(p[0], p[2], p[3]) for p in parts]
    v_specs = [kv_spec(p[1], p[2], p[3]) for p in parts]
    n_steps = grid[1]
    return pl.pallas_call(
        functools.partial(_attn_kernel, n_parts=len(parts), group=group,
                          first_valid_step=tuple(p[4] for p in parts)),
        grid=grid,
        in_specs=[pl.BlockSpec((q_rows, q_dim), q_map)] + k_specs + v_specs + [
            pl.BlockSpec(offsets.shape, lambda b, c: (0, 0, 0))],
        out_specs=pl.BlockSpec((q_rows, q_dim), lambda b, c: (b * n_steps + c, 0)),
        out_shape=jax.ShapeDtypeStruct((n_out_rows, q_dim), BF16),
        compiler_params=_params("parallel", "arbitrary"),
        name="swa_attention",
    )(q, *[p[0] for p in parts], *[p[1] for p in parts], offsets)


def _split3(x):
    hi = x.astype(BF16)
    r1 = x - hi.astype(F32)
    mid = r1.astype(BF16)
    lo = (r1 - mid.astype(F32)).astype(BF16)
    return hi, mid, lo


def _wkv_kernel(r_ref, k_ref, v_ref, lw_ref, al_ref, gate_ref, kk_ref, ka_ref, rk_ref, lnw_ref,
                lnb_ref, s0_ref, o_ref, s_out_ref, h_ref):
    L = WKV_CHUNK
    L2 = 2 * L
    valid = r_ref.shape[0]
    n_pairs = r_ref.shape[1] // LANES
    c_idx = pl.program_id(1)
    half = RWKV_HEAD

    lane = lax.broadcasted_iota(jnp.int32, (1, LANES), 1)
    head0 = lane < half
    row2 = lax.broadcasted_iota(jnp.int32, (L2, L2), 0)
    col2 = lax.broadcasted_iota(jnp.int32, (L2, L2), 1)
    strict_lower = row2 > col2
    lower = row2 >= col2
    eye2 = (row2 == col2).astype(F32)
    tril = (lax.broadcasted_iota(jnp.int32, (L, L), 0) >= lax.broadcasted_iota(jnp.int32, (L, L), 1)
            ).astype(BF16)
    krow = lax.broadcasted_iota(jnp.int32, (LANES, LANES), 0)
    kcol = lax.broadcasted_iota(jnp.int32, (LANES, LANES), 1)
    same_head = (krow < half) == (kcol < half)
    eye_k = (krow == kcol).astype(F32)

    def load(ref, cs):
        x = ref[:, cs]
        if valid < L:
            x = jnp.concatenate([x, jnp.zeros((L - valid, LANES), x.dtype)], axis=0)
        return x

    def seg_sum(x):
        s0 = jnp.sum(jnp.where(head0, x, 0.0), axis=-1, keepdims=True)
        s1 = jnp.sum(jnp.where(head0, 0.0, x), axis=-1, keepdims=True)
        return jnp.where(head0, s0, s1)

    def stack2(x):
        return jnp.concatenate([jnp.where(head0, x, 0.0), jnp.where(head0, 0.0, x)], axis=0)

    def fold(m):
        return m[:L] + m[L:]

    @pl.when(c_idx == 0)
    def _():
        for p in range(n_pairs):
            s0 = jnp.concatenate([s0_ref[2 * p].T, s0_ref[2 * p + 1].T], axis=0)
            h_ref[p] = jnp.where(same_head, jnp.concatenate([s0, s0], axis=1), 0.0)

    pairs = range(n_pairs)
    cols = [slice(p * LANES, (p + 1) * LANES) for p in pairs]

    r = [load(r_ref, cs) for cs in cols]
    v = [load(v_ref, cs) for cs in cols]
    lw = [load(lw_ref, cs) for cs in cols]
    c3 = [_dot(tril, jnp.concatenate(_split3(x), axis=1)) for x in lw]
    c = [x[:, :LANES] + x[:, LANES:2 * LANES] + x[:, 2 * LANES:] for x in c3]

    k, a, b = [], [], []
    for p in pairs:
        k_raw = load(k_ref, cols[p])
        al = load(al_ref, cols[p])
        kk = k_raw * kk_ref[:, cols[p]]
        kk = kk / jnp.maximum(jnp.sqrt(seg_sum(kk * kk)), 1e-12)
        k.append(k_raw * (1.0 + (al - 1.0) * ka_ref[:, cols[p]]))
        a.append(-kk)
        b.append(kk * al)

    at2, v2, rt, bc2, kc2, d_end, gram = [], [], [], [], [], [], []
    for p in pairs:
        c_last = c[p][L - 1:L]
        e_neg = jnp.exp(-c[p])
        e_end = jnp.exp(c_last - c[p])
        at2.append(stack2(a[p] * jnp.exp(c[p] - lw[p])))
        rt.append(r[p] * jnp.exp(c[p]))
        v2.append(stack2(v[p]))
        bc2.append(stack2(b[p] * e_end))
        kc2.append(stack2(k[p] * e_end))
        d_end.append(jnp.exp(c_last))
        gram.append(_dot_nt(
            jnp.concatenate([at2[p], stack2(rt[p])], axis=0).astype(BF16),
            jnp.concatenate([stack2(b[p] * e_neg), stack2(k[p] * e_neg)], axis=0).astype(BF16)))

    n_mat = [jnp.where(strict_lower, g[:L2, :L2], 0.0) for g in gram]
    akv2 = [_dot(jnp.where(strict_lower, gram[p][:L2, L2:], 0.0).astype(BF16), v2[p].astype(BF16))
            for p in pairs]
    lhs_y = [jnp.concatenate([fold(jnp.where(lower, g[L2:, :L2], 0.0)),
                              fold(jnp.where(lower, g[L2:, L2:], 0.0))], axis=1) for g in gram]

    size = 1
    t_inv = [eye2 for _ in pairs]
    while size < L:
        quadrant = ((row2 // (2 * size)) == (col2 // (2 * size))) & \
                   ((row2 // size) % 2 == 1) & ((col2 // size) % 2 == 0)
        n_c = [jnp.where(quadrant, n, 0.0) for n in n_mat]
        if size == 1:
            t_inv = [t + n for t, n in zip(t_inv, n_c)]
        else:
            t_bf = [t.astype(BF16) for t in t_inv]
            tn = [_dot(t, n.astype(BF16)).astype(BF16) for t, n in zip(t_bf, n_c)]
            t_inv = [t + _dot(x, tb) for t, x, tb in zip(t_inv, tn, t_bf)]
        size *= 2

    u2 = [_dot(t_inv[p].astype(BF16), jnp.concatenate([at2[p], akv2[p]], axis=1).astype(BF16))
          for p in pairs]
    yz = []
    for p in pairs:
        z = jnp.concatenate(
            [u2[p], jnp.concatenate([jnp.zeros((L2, LANES), F32), v2[p]], axis=1)], axis=0).astype(BF16)
        lhs_h = jnp.concatenate([bc2[p], kc2[p]], axis=0).T
        yz.append(_dot(jnp.concatenate([lhs_y[p], lhs_h], axis=0).astype(BF16), z))

    hh = []
    for p in pairs:
        q_hat = rt[p] + yz[p][:L, :LANES]
        m_mat = eye_k * d_end[p] + yz[p][L:, :LANES]
        hh.append(_dot(jnp.concatenate([m_mat, q_hat], axis=0).astype(BF16), h_ref[p].astype(BF16)))

    for p in pairs:
        h_ref[p] = jnp.where(same_head, hh[p][:LANES] + yz[p][L:, LANES:], 0.0)
        y = hh[p][LANES:] + yz[p][:L, LANES:]
        mean = seg_sum(y) * (1.0 / half)
        yc = y - mean
        var = seg_sum(yc * yc) * (1.0 / half)
        yn = yc * lax.rsqrt(var + GN_EPS) * lnw_ref[:, cols[p]] + lnb_ref[:, cols[p]]
        bonus = seg_sum(r[p] * k[p] * rk_ref[:, cols[p]]) * v[p]
        out = yn + bonus
        o_ref[:, cols[p]] = (out[:valid] * gate_ref[:, cols[p]]).astype(o_ref.dtype)

    @pl.when(c_idx == pl.num_programs(1) - 1)
    def _():
        for p in range(n_pairs):
            h_last = h_ref[p]
            s_out_ref[2 * p] = h_last[:half, :half].T
            s_out_ref[2 * p + 1] = h_last[half:, half:].T


def _wkv(rkv, lw, al, gate, k_k, k_a, r_k, ln_w, ln_b, state, *, n_streams, n_chunks, rows, first_block):
    d = lw.shape[1]
    n_pairs = d // LANES

    def block(b, c):
        return first_block + b * n_chunks + c

    seq = pl.BlockSpec((rows, d), lambda b, c: (block(b, c), 0))
    rkv_specs = [pl.BlockSpec((None, rows, d), lambda b, c, s=s: (s, block(b, c), 0)) for s in range(3)]
    vec = pl.BlockSpec((1, d), lambda b, c: (0, 0))
    st = pl.BlockSpec((None,) + state.shape[1:], lambda b, c: (b, 0, 0, 0))
    return pl.pallas_call(
        _wkv_kernel,
        grid=(n_streams, n_chunks),
        in_specs=rkv_specs + [seq] * 3 + [vec] * 5 + [st],
        out_specs=[pl.BlockSpec((rows, d), lambda b, c: (b * n_chunks + c, 0)), st],
        out_shape=[jax.ShapeDtypeStruct((n_streams * n_chunks * rows, d), BF16),
                   jax.ShapeDtypeStruct(state.shape, F32)],
        scratch_shapes=[pltpu.VMEM((n_pairs, LANES, LANES), F32)],
        compiler_params=_params("parallel", "arbitrary"),
        name="rwkv7_wkv",
    )(rkv, rkv, rkv, lw, al, gate, *[x.reshape(1, d) for x in (k_k, k_a, r_k, ln_w, ln_b)], state)


def _pad_lora(w_in, w_out):
    rank = w_in.shape[1]
    pad = (-rank) % LORA_PAD
    return (jnp.pad(w_in, ((0, 0), (0, pad))).astype(BF16),
            jnp.pad(w_out, ((0, pad), (0, 0))).astype(BF16))


def kernel(x_prompt, x_sample, cache_k, cache_v, state_shift, state_wkv, norm_g, ffn_w_gate, ffn_w_up, ffn_w_down, rel_table, att_w_qkv, att_b_qkv, att_w_o, att_sinks, rwkv_mu, rwkv_w_r, rwkv_w_k, rwkv_w_v, rwkv_w_o, rwkv_w0, rwkv_w1, rwkv_w2, rwkv_a0, rwkv_a1, rwkv_a2, rwkv_g1, rwkv_g2, rwkv_k_k, rwkv_k_a, rwkv_r_k, rwkv_ln_w, rwkv_ln_b):
    n_bp, t_p, d = x_prompt.shape
    n_bs, t_s, _ = x_sample.shape
    n_p = n_bp * t_p
    n_s = n_bs * t_s
    depth = norm_g.shape[0]
    q_dim = att_w_o.shape[1]
    kv_dim = N_KV_HEADS * HEAD_DIM
    assert n_p % ROW_TILE == 0 and n_s % ROW_TILE == 0 and ROW_TILE % t_s == 0 and t_p % CHUNK == 0
    assert t_p % WKV_CHUNK == 0 and t_s <= WKV_CHUNK and t_p % t_s == 0
    assert ROW_TILE % RWKV_ROW_TILE == 0 and RWKV_ROW_TILE % t_s == 0

    ffn_w = (ffn_w_gate.astype(BF16), ffn_w_up.astype(BF16), ffn_w_down.astype(BF16))

    x = (x_prompt.reshape(n_p, d), x_sample.reshape(n_s, d))
    new_k, new_v, new_shift, new_wkv = [], [], [], []
    for i in range(depth):
        g = norm_g[i]
        j = i // 2
        x = _ffn_block(x, g[0], g[1], ffn_w, i, 0)
        if i % 2 == 0:
            q, k, v = _qkv_proj(x, g[2], att_w_qkv[j].astype(BF16), att_b_qkv[j], q_dim, kv_dim)

            n_back = WINDOW // CHUNK
            n_c = t_p // CHUNK
            parts = [(k, v, CHUNK, (lambda b, c, back=back: (b * n_c + jnp.maximum(c - back, 0), 0)), back)
                     for back in range(n_back, -1, -1)]
            o_p = _attention(q, parts, _logit_offsets(rel_table, att_sinks[j], CHUNK, (n_back + 1) * CHUNK),
                             grid=(n_bp, n_c), q_rows=CHUNK, q_map=lambda b, c: (b * n_c + c, 0),
                             n_out_rows=n_p)
            first = n_p // t_s
            parts = [(cache_k[j].reshape(n_bs, WINDOW, kv_dim), cache_v[j].reshape(n_bs, WINDOW, kv_dim),
                      WINDOW, (lambda b, c: (b, 0, 0)), 0),
                     (k, v, t_s, (lambda b, c: (first + b, 0)), 0)]
            o_s = _attention(q, parts, _logit_offsets(rel_table, att_sinks[j], t_s, WINDOW + t_s),
                             grid=(n_bs, 1), q_rows=t_s, q_map=lambda b, c: (first + b, 0),
                             n_out_rows=n_s)
            x = _out_proj(o_p, o_s, att_w_o[j].astype(BF16), g[3], x)

            def kv_rows(a):
                newest = jnp.stack([a[(b + 1) * t_p - WINDOW:(b + 1) * t_p] for b in range(n_bp)])
                return (newest.reshape(n_bp, WINDOW, N_KV_HEADS, HEAD_DIM),
                        a[n_p:].reshape(n_bs, t_s, N_KV_HEADS, HEAD_DIM))

            new_k.append(kv_rows(k))
            new_v.append(kv_rows(v))
        else:
            first_rows = jnp.broadcast_to(state_shift[j], (n_bs, t_s, d)).reshape(n_s, d)
            streams = (n_p, t_p, t_s)
            mu = rwkv_mu[j]
            rkv = _rkv_proj(x, g[2], first_rows, streams, jnp.stack([mu[0], mu[2], mu[3]]),
                            jnp.stack([rwkv_w_r[j], rwkv_w_k[j], rwkv_w_v[j]]).astype(BF16))
            w1, w2 = _pad_lora(rwkv_w1[j], rwkv_w2[j])
            a1, a2 = _pad_lora(rwkv_a1[j], rwkv_a2[j])
            g1, g2 = _pad_lora(rwkv_g1[j], rwkv_g2[j])
            lw, al, gate, h_tail = _lora_branches(x, g[2], first_rows, streams,
                                                  jnp.stack([mu[1], mu[4], mu[5]]), rwkv_w0[j], w1, w2,
                                                  rwkv_a0[j], a1, a2, g1, g2)
            vecs = (rwkv_k_k[j], rwkv_k_a[j], rwkv_r_k[j].reshape(d), rwkv_ln_w[j], rwkv_ln_b[j])
            zero_state = jnp.zeros((n_bp, d // RWKV_HEAD, RWKV_HEAD, RWKV_HEAD), F32)
            o_p, wkv_p = _wkv(rkv, lw, al, gate, *vecs, zero_state, n_streams=n_bp,
                              n_chunks=t_p // WKV_CHUNK, rows=WKV_CHUNK, first_block=0)
            o_s, wkv_s = _wkv(rkv, lw, al, gate, *vecs, state_wkv[j], n_streams=n_bs,
                              n_chunks=1, rows=t_s, first_block=n_p // t_s)
            x = _out_proj(o_p, o_s, rwkv_w_o[j].astype(BF16), g[3], x)
            groups = t_p // t_s
            new_shift.append((h_tail[groups - 1:n_p // t_s:groups].reshape(n_bp, 1, d),
                              h_tail[n_p // t_s:].reshape(n_bs, 1, d)))
            new_wkv.append((wkv_p, wkv_s))
        x = _ffn_block(x, g[4], g[5], ffn_w, i, 1, split_rows=n_p if i == depth - 1 else None)

    def both(pairs):
        return jnp.stack([p[0] for p in pairs]), jnp.stack([p[1] for p in pairs])

    k_prompt, k_sample = both(new_k)
    v_prompt, v_sample = both(new_v)
    shift_prompt, shift_sample = both(new_shift)
    wkv_prompt, wkv_sample = both(new_wkv)
    return (x[0].reshape(n_bp, t_p, d), x[1].reshape(n_bs, t_s, d),
            k_prompt, v_prompt, k_sample, v_sample,
            shift_prompt, wkv_prompt, shift_sample, wkv_sample)
```

```python
import functools
import math

import jax
import jax.numpy as jnp
from jax import lax
from jax.experimental import pallas as pl
from jax.experimental.pallas import tpu as pltpu

F32 = jnp.float32
BF16 = jnp.bfloat16

HEAD_DIM = 64
N_KV_HEADS = 4
CHUNK = 64
WINDOW = 128
N_BUCKETS = 32
MAX_DISTANCE = 128
RWKV_HEAD = 64
RMS_EPS = 1e-6
GN_EPS = RWKV_HEAD * 1e-5
NEG_INF = -1e30
LORA_PAD = 128

LANES = 128
KEY_PAD = 256
VMEM_LIMIT_BYTES = 56 * 1024 * 1024
ROW_TILE = 512
RWKV_ROW_TILE = 256
FF_TILE = 512
WKV_CHUNK = 64


def _params(*semantics):
    return pltpu.CompilerParams(dimension_semantics=semantics,
                                vmem_limit_bytes=VMEM_LIMIT_BYTES)


def _rms(x):
    return x * lax.rsqrt(jnp.mean(jnp.square(x), axis=-1, keepdims=True) + RMS_EPS)


def _dot(a, b):
    return jnp.dot(a, b, preferred_element_type=F32)


def _dot_nt(a, b):
    return lax.dot_general(a, b, (((1,), (1,)), ((), ())), preferred_element_type=F32)


def _two_part_specs(d, prompt_tiles):
    return [pl.BlockSpec((ROW_TILE, d), lambda i, *_: (jnp.minimum(i, prompt_tiles - 1), 0)),
            pl.BlockSpec((ROW_TILE, d), lambda i, *_: (jnp.maximum(i - prompt_tiles, 0), 0))]


def _ffn_kernel(*refs, n_x, n_out, prompt_tiles):
    x_refs = refs[:n_x]
    g0_ref, g1_ref, wgu_ref, wd_ref = refs[n_x:n_x + 4]
    o_refs = refs[n_x + 4:n_x + 4 + n_out]
    h_ref = refs[n_x + 4 + n_out]
    i = pl.program_id(0)
    f = pl.program_id(1)

    def partial_down(h):
        ab = _dot(h, wgu_ref[...])
        a = ab[:, :FF_TILE]
        b = ab[:, FF_TILE:]
        return _dot((a * jax.nn.sigmoid(a) * b).astype(BF16), wd_ref[...])

    def body(x_ref, o_ref):
        last = pl.num_programs(1) - 1

        @pl.when(f == 0)
        def _():
            h = (_rms(x_ref[...]) * g0_ref[...]).astype(BF16)
            h_ref[...] = h
            o_ref[...] = partial_down(h)

        @pl.when((f > 0) & (f < last))
        def _():
            o_ref[...] += partial_down(h_ref[...])

        @pl.when(f == last)
        def _():
            y = o_ref[...] + partial_down(h_ref[...])
            o_ref[...] = x_ref[...] + 0.5 * (_rms(y) * g1_ref[...])

    if n_x == 1 and n_out == 1:
        body(x_refs[0], o_refs[0])
    else:
        pl.when(i < prompt_tiles)(lambda: body(x_refs[0], o_refs[0]))
        pl.when(i >= prompt_tiles)(lambda: body(x_refs[-1], o_refs[-1]))


def _tile_gate_up(w_gate, w_up):
    n_l, n_h, d, d_ff = w_gate.shape

    def tiles(w):
        return w.astype(BF16).reshape(n_l, n_h, d, d_ff // FF_TILE, FF_TILE)

    return jnp.transpose(jnp.concatenate([tiles(w_gate), tiles(w_up)], axis=-1), (0, 1, 3, 2, 4))


def _ffn_block(x, g0, g1, w, layer, half, split_rows=None):
    xs = x if isinstance(x, tuple) else (x,)
    d = xs[0].shape[1]
    n = sum(a.shape[0] for a in xs)
    d_ff = w[1].shape[-2]
    assert d_ff // FF_TILE >= 2
    row = pl.BlockSpec((ROW_TILE, d), lambda i, f: (i, 0))
    prompt_tiles = n // ROW_TILE
    x_specs, out_specs, out_shape = [row], [row], [jax.ShapeDtypeStruct((n, d), F32)]
    if len(xs) == 2:
        prompt_tiles = xs[0].shape[0] // ROW_TILE
        x_specs = _two_part_specs(d, prompt_tiles)
    if split_rows is not None:
        prompt_tiles = split_rows // ROW_TILE
        out_specs = _two_part_specs(d, prompt_tiles)
        out_shape = [jax.ShapeDtypeStruct((split_rows, d), F32),
                     jax.ShapeDtypeStruct((n - split_rows, d), F32)]
    out = pl.pallas_call(
        functools.partial(_ffn_kernel, n_x=len(xs), n_out=len(out_shape), prompt_tiles=prompt_tiles),
        grid=(n // ROW_TILE, d_ff // FF_TILE),
        in_specs=x_specs + [
            pl.BlockSpec((1, d), lambda i, f: (0, 0)),
            pl.BlockSpec((1, d), lambda i, f: (0, 0)),
            pl.BlockSpec((None, None, None, d, 2 * FF_TILE), lambda i, f: (layer, half, f, 0, 0)),
            pl.BlockSpec((None, None, FF_TILE, d), lambda i, f: (layer, half, f, 0)),
        ],
        out_specs=out_specs,
        out_shape=out_shape,
        scratch_shapes=[pltpu.VMEM((ROW_TILE, d), BF16)],
        compiler_params=_params("arbitrary", "arbitrary"),
        name="ffn_block",
    )(*xs, g0.reshape(1, d), g1.reshape(1, d), *w)
    return tuple(out) if split_rows is not None else out[0]


def _normed_and_shifted(x_ref, g_ref, first_ref, carry_ref, streams):
    prompt_tiles, prompt_len, sample_len = streams
    tile = x_ref.shape[0]
    i = pl.program_id(0)

    @pl.when(i == 0)
    def _():
        carry_ref[...] = jnp.zeros_like(carry_ref)

    h = _rms(x_ref[...]) * g_ref[...]
    rows = lax.broadcasted_iota(jnp.int32, (tile, 1), 0)
    prev = jnp.where(rows == 0, carry_ref[...], pltpu.roll(h, 1, axis=0))
    carry_ref[...] = h[tile - 1:tile]
    prompt_start = (rows + i * tile) % prompt_len == 0
    sample_start = rows % sample_len == 0
    return h, jnp.where(i < prompt_tiles, jnp.where(prompt_start, 0.0, prev),
                        jnp.where(sample_start, first_ref[...], prev))


def _shift_specs(d, tile, n_prompt_rows):
    prompt_tiles = n_prompt_rows // tile
    return [pl.BlockSpec((tile, d), lambda i: (i, 0)),
            pl.BlockSpec((1, d), lambda i: (0, 0)),
            pl.BlockSpec((tile, d), lambda i: (jnp.maximum(i - prompt_tiles, 0), 0))]


def _qkv_kernel(x_ref, g_ref, w_ref, b_ref, q_ref, k_ref, v_ref):
    h = (_rms(x_ref[...]) * g_ref[...]).astype(BF16)
    y = _dot(h, w_ref[...]) + b_ref[...]
    q_dim = q_ref.shape[1]
    kv_dim = k_ref.shape[1]
    q_ref[...] = y[:, :q_dim].astype(q_ref.dtype)
    k_ref[...] = y[:, q_dim:q_dim + kv_dim]
    v_ref[...] = y[:, q_dim + kv_dim:]


def _qkv_proj(x, g, w, b, q_dim, kv_dim):
    n, d = x.shape
    m = w.shape[1]

    def rows(width):
        return pl.BlockSpec((ROW_TILE, width), lambda i: (i, 0))

    return pl.pallas_call(
        _qkv_kernel,
        grid=(n // ROW_TILE,),
        in_specs=[rows(d), pl.BlockSpec((1, d), lambda i: (0, 0)),
                  pl.BlockSpec((d, m), lambda i: (0, 0)), pl.BlockSpec((1, m), lambda i: (0, 0))],
        out_specs=[rows(q_dim), rows(kv_dim), rows(kv_dim)],
        out_shape=[jax.ShapeDtypeStruct((n, q_dim), BF16), jax.ShapeDtypeStruct((n, kv_dim), F32),
                   jax.ShapeDtypeStruct((n, kv_dim), F32)],
        compiler_params=_params("parallel"),
        name="qkv_proj",
    )(x, g.reshape(1, d), w, b.reshape(1, m))


def _rkv_kernel(x_ref, g_ref, first_ref, mu_ref, w_ref, o_ref, carry_ref, *, streams):
    h, hp = _normed_and_shifted(x_ref, g_ref, first_ref, carry_ref, streams)
    dx = hp - h
    for s in range(w_ref.shape[0]):
        o_ref[s] = _dot((h + dx * mu_ref[s]).astype(BF16), w_ref[s])


def _rkv_proj(x, g, first_rows, streams, mu, w):
    n, d = x.shape
    k = w.shape[0]
    m = w.shape[2]
    n_prompt_rows, prompt_len, sample_len = streams
    return pl.pallas_call(
        functools.partial(_rkv_kernel, streams=(n_prompt_rows // RWKV_ROW_TILE, prompt_len, sample_len)),
        grid=(n // RWKV_ROW_TILE,),
        in_specs=_shift_specs(d, RWKV_ROW_TILE, n_prompt_rows) + [
            pl.BlockSpec((k, 1, d), lambda i: (0, 0, 0)),
            pl.BlockSpec((k, d, m), lambda i: (0, 0, 0), pipeline_mode=pl.Buffered(1))],
        out_specs=pl.BlockSpec((k, RWKV_ROW_TILE, m), lambda i: (0, i, 0)),
        out_shape=jax.ShapeDtypeStruct((k, n, m), F32),
        scratch_shapes=[pltpu.VMEM((1, d), F32)],
        compiler_params=_params("arbitrary"),
        name="rkv_proj",
    )(x, g.reshape(1, d), first_rows, mu.reshape(k, 1, d), w)


def _lora_kernel(x_ref, g_ref, first_ref, mu_ref, w0_ref, w1_ref, w2_ref, a0_ref, a1_ref, a2_ref,
                 g1_ref, g2_ref, lw_ref, al_ref, gate_ref, tail_ref, carry_ref, *, streams):
    h, hp = _normed_and_shifted(x_ref, g_ref, first_ref, carry_ref, streams)
    dx = hp - h
    sample_len = streams[2]
    tail_ref[...] = jnp.concatenate(
        [h[(t + 1) * sample_len - 1:(t + 1) * sample_len] for t in range(tail_ref.shape[0])], axis=0)

    xw = (h + dx * mu_ref[0]).astype(BF16)
    t = jnp.tanh(_dot(xw, w1_ref[...])).astype(BF16)
    z = -(w0_ref[...] + _dot(t, w2_ref[...]))
    softplus = jnp.maximum(z, 0.0) + jnp.log(1.0 + jnp.exp(-jnp.abs(z)))
    lw_ref[...] = -jnp.exp(-softplus - 0.5)

    xa = (h + dx * mu_ref[1]).astype(BF16)
    u = _dot(xa, a1_ref[...]).astype(BF16)
    al_ref[...] = jax.nn.sigmoid(a0_ref[...] + _dot(u, a2_ref[...]))

    xg = (h + dx * mu_ref[2]).astype(BF16)
    s = jax.nn.sigmoid(_dot(xg, g1_ref[...])).astype(BF16)
    gate_ref[...] = _dot(s, g2_ref[...])


def _lora_branches(x, g, first_rows, streams, mu, w0, w1, w2, a0, a1, a2, g1, g2):
    n, d = x.shape
    n_prompt_rows, prompt_len, sample_len = streams
    tile = RWKV_ROW_TILE
    row = pl.BlockSpec((tile, d), lambda i: (i, 0))
    tail = pl.BlockSpec((tile // sample_len, d), lambda i: (i, 0))

    def whole(a):
        return pl.BlockSpec(a.shape, lambda i: (0,) * a.ndim)

    mu = mu.reshape(3, 1, d)
    w0 = w0.reshape(1, d)
    a0 = a0.reshape(1, d)
    consts = (mu, w0, w1, w2, a0, a1, a2, g1, g2)
    out = jax.ShapeDtypeStruct((n, d), F32)
    return pl.pallas_call(
        functools.partial(_lora_kernel, streams=(n_prompt_rows // tile, prompt_len, sample_len)),
        grid=(n // tile,),
        in_specs=_shift_specs(d, tile, n_prompt_rows) + [whole(c) for c in consts],
        out_specs=[row, row, row, tail],
        out_shape=[out, out, out, jax.ShapeDtypeStruct((n // sample_len, d), F32)],
        scratch_shapes=[pltpu.VMEM((1, d), F32)],
        compiler_params=_params("arbitrary"),
        name="rwkv_lora",
    )(x, g.reshape(1, d), first_rows, *consts)


def _out_proj_kernel(op_ref, os_ref, w_ref, g_ref, x_ref, y_ref, *, prompt_tiles):
    o = jnp.where(pl.program_id(0) < prompt_tiles, op_ref[...], os_ref[...])
    y = _dot(o, w_ref[...])
    y_ref[...] = x_ref[...] + _rms(y) * g_ref[...]


def _out_proj(o_prompt, o_sample, w, g, xres):
    n, d = xres.shape
    k = o_prompt.shape[1]
    prompt_tiles = o_prompt.shape[0] // ROW_TILE
    return pl.pallas_call(
        functools.partial(_out_proj_kernel, prompt_tiles=prompt_tiles),
        grid=(n // ROW_TILE,),
        in_specs=_two_part_specs(k, prompt_tiles) + [
            pl.BlockSpec((k, d), lambda i: (0, 0)),
            pl.BlockSpec((1, d), lambda i: (0, 0)),
            pl.BlockSpec((ROW_TILE, d), lambda i: (i, 0))],
        out_specs=pl.BlockSpec((ROW_TILE, d), lambda i: (i, 0)),
        out_shape=jax.ShapeDtypeStruct((n, d), F32),
        compiler_params=_params("arbitrary"),
        name="out_proj",
    )(o_prompt, o_sample, w, g.reshape(1, d), xres)


def _t5_bucket(rel):
    nb = N_BUCKETS // 2
    max_exact = nb // 2
    offset = jnp.where(rel > 0, nb, 0)
    n = jnp.abs(rel)
    nf = jnp.maximum(n, 1).astype(F32)
    large = max_exact + (jnp.log(nf / max_exact) / math.log(MAX_DISTANCE / max_exact)
                         * (nb - max_exact)).astype(jnp.int32)
    large = jnp.minimum(large, nb - 1)
    return offset + jnp.where(n < max_exact, n, large)


def _bias_kernel(tt_ref, b_ref, o_ref):
    onehot = (lax.broadcasted_iota(jnp.int32, (tt_ref.shape[1], b_ref.shape[1]), 0) == b_ref[...]).astype(F32)
    o_ref[...] = jnp.dot(tt_ref[...], onehot, preferred_element_type=F32,
                         precision=lax.Precision.HIGHEST)


def _logit_offsets(table, sinks, n_q, n_k):
    assert n_k < KEY_PAD
    rel = (jnp.arange(n_k, dtype=jnp.int32)[None, :] - WINDOW - jnp.arange(n_q, dtype=jnp.int32)[:, None])
    rows = jnp.concatenate([_t5_bucket(rel).astype(jnp.int32),
                            jnp.full((n_q, 1), N_BUCKETS, jnp.int32),
                            jnp.full((n_q, KEY_PAD - n_k - 1), N_BUCKETS + 1, jnp.int32)], axis=1)
    n_heads = table.shape[1]
    values = jnp.concatenate([table.T, sinks.astype(F32)[:, None],
                              jnp.full((n_heads, 1), NEG_INF, F32)], axis=1)
    out = pl.pallas_call(
        _bias_kernel,
        out_shape=jax.ShapeDtypeStruct((n_heads, n_q * KEY_PAD), F32),
        name="logit_offsets",
    )(values, rows.reshape(1, n_q * KEY_PAD))
    return out.reshape(n_heads, n_q, KEY_PAD)


def _attn_kernel(*refs, n_parts, group, first_valid_step):
    q_ref = refs[0]
    k_refs = refs[1:1 + n_parts]
    v_refs = refs[1 + n_parts:1 + 2 * n_parts]
    off_ref, o_ref = refs[1 + 2 * n_parts:]
    n_q = q_ref.shape[0]
    step = pl.program_id(1)
    scale = HEAD_DIM ** -0.5
    part_rows = [r.shape[0] for r in k_refs]
    n_k = sum(part_rows)
    heads = range(N_KV_HEADS)
    tiles = range(group // 2)

    col = lax.broadcasted_iota(jnp.int32, (1, KEY_PAD), 1)
    hidden = None
    start = 0
    for p, rows in enumerate(part_rows):
        if first_valid_step[p] > 0:
            h_p = (col >= start) & (col < start + rows) & (step < first_valid_step[p])
            hidden = h_p if hidden is None else hidden | h_p
        start += rows

    zeros = jnp.zeros((KEY_PAD, HEAD_DIM), F32)
    ones = jnp.ones((KEY_PAD, HEAD_DIM), F32)

    def head_rows(refs_, hk):
        kv_cols = slice(hk * HEAD_DIM, (hk + 1) * HEAD_DIM)
        return jnp.concatenate([r[:, kv_cols] for r in refs_] + [zeros[:KEY_PAD - n_k]], axis=0)

    scores = []
    for hk in heads:
        k = head_rows(k_refs, hk)
        k2 = jnp.concatenate([jnp.concatenate([k, zeros], axis=1),
                              jnp.concatenate([zeros, k], axis=1)], axis=0).astype(BF16)
        q_rows = jnp.concatenate(
            [q_ref[:, (hk * group + 2 * t) * HEAD_DIM:(hk * group + 2 * t + 2) * HEAD_DIM] for t in tiles],
            axis=0)
        off = jnp.concatenate(
            [jnp.concatenate([off_ref[hk * group + 2 * t], off_ref[hk * group + 2 * t + 1]], axis=1)
             for t in tiles], axis=0)
        s = _dot_nt(q_rows, k2) * scale + off
        if hidden is not None:
            s = jnp.where(jnp.concatenate([hidden, hidden], axis=1), NEG_INF, s)
        scores.append(s)

    expo = []
    for hk in heads:
        s = scores[hk]
        m0 = jnp.max(s[:, :KEY_PAD], axis=-1, keepdims=True)
        m1 = jnp.max(s[:, KEY_PAD:], axis=-1, keepdims=True)
        expo.append(jnp.concatenate([jnp.exp(s[:, :KEY_PAD] - m0), jnp.exp(s[:, KEY_PAD:] - m1)],
                                    axis=1).astype(BF16))

    for hk in heads:
        v = head_rows(v_refs, hk)
        rhs = jnp.concatenate([jnp.concatenate([v, zeros, ones, zeros], axis=1),
                               jnp.concatenate([zeros, v, zeros, ones], axis=1)], axis=0).astype(BF16)
        acc = _dot(expo[hk], rhs)
        o = acc[:, :LANES] / acc[:, LANES:]
        for t in tiles:
            first_head = hk * group + 2 * t
            o_ref[:, first_head * HEAD_DIM:(first_head + 2) * HEAD_DIM] = (
                o[t * n_q:(t + 1) * n_q].astype(o_ref.dtype))


def _attention(q, parts, offsets, *, grid, q_rows, q_map, n_out_rows):
    q_dim = q.shape[1]
    n_heads = q_dim // HEAD_DIM
    group = n_heads // N_KV_HEADS
    assert offsets.shape == (n_heads, q_rows, KEY_PAD) and group % 2 == 0

    def kv_spec(arr, rows, imap):
        if arr.ndim == 3:
            return pl.BlockSpec((None, rows, arr.shape[2]), imap)
        return pl.BlockSpec((rows, arr.shape[1]), imap)

    k_specs = [kv_spec(p[0], p[2], p[3]) for p in parts]
    v_specs = [kv_spec(p[1], p[2], p[3]) for p in parts]
    n_steps = grid[1]
    return pl.pallas_call(
        functools.partial(_attn_kernel, n_parts=len(parts), group=group,
                          first_valid_step=tuple(p[4] for p in parts)),
        grid=grid,
        in_specs=[pl.BlockSpec((q_rows, q_dim), q_map)] + k_specs + v_specs + [
            pl.BlockSpec(offsets.shape, lambda b, c: (0, 0, 0))],
        out_specs=pl.BlockSpec((q_rows, q_dim), lambda b, c: (b * n_steps + c, 0)),
        out_shape=jax.ShapeDtypeStruct((n_out_rows, q_dim), BF16),
        compiler_params=_params("parallel", "arbitrary"),
        name="swa_attention",
    )(q, *[p[0] for p in parts], *[p[1] for p in parts], offsets)


def _split3(x):
    hi = x.astype(BF16)
    r1 = x - hi.astype(F32)
    mid = r1.astype(BF16)
    lo = (r1 - mid.astype(F32)).astype(BF16)
    return hi, mid, lo


def _wkv_kernel(r_ref, k_ref, v_ref, lw_ref, al_ref, gate_ref, kk_ref, ka_ref, rk_ref, lnw_ref,
                lnb_ref, s0_ref, o_ref, s_out_ref, h_ref):
    L = WKV_CHUNK
    L2 = 2 * L
    valid = r_ref.shape[0]
    n_pairs = r_ref.shape[1] // LANES
    c_idx = pl.program_id(1)
    half = RWKV_HEAD

    lane = lax.broadcasted_iota(jnp.int32, (1, LANES), 1)
    head0 = lane < half
    row2 = lax.broadcasted_iota(jnp.int32, (L2, L2), 0)
    col2 = lax.broadcasted_iota(jnp.int32, (L2, L2), 1)
    strict_lower = row2 > col2
    lower = row2 >= col2
    eye2 = (row2 == col2).astype(F32)
    tril = (lax.broadcasted_iota(jnp.int32, (L, L), 0) >= lax.broadcasted_iota(jnp.int32, (L, L), 1)
            ).astype(BF16)
    krow = lax.broadcasted_iota(jnp.int32, (LANES, LANES), 0)
    kcol = lax.broadcasted_iota(jnp.int32, (LANES, LANES), 1)
    same_head = (krow < half) == (kcol < half)
    eye_k = (krow == kcol).astype(F32)

    def load(ref, cs):
        x = ref[:, cs]
        if valid < L:
            x = jnp.concatenate([x, jnp.zeros((L - valid, LANES), x.dtype)], axis=0)
        return x

    def seg_sum(x):
        s0 = jnp.sum(jnp.where(head0, x, 0.0), axis=-1, keepdims=True)
        s1 = jnp.sum(jnp.where(head0, 0.0, x), axis=-1, keepdims=True)
        return jnp.where(head0, s0, s1)

    def stack2(x):
        return jnp.concatenate([jnp.where(head0, x, 0.0), jnp.where(head0, 0.0, x)], axis=0)

    def fold(m):
        return m[:L] + m[L:]

    @pl.when(c_idx == 0)
    def _():
        for p in range(n_pairs):
            s0 = jnp.concatenate([s0_ref[2 * p].T, s0_ref[2 * p + 1].T], axis=0)
            h_ref[p] = jnp.where(same_head, jnp.concatenate([s0, s0], axis=1), 0.0)

    pairs = range(n_pairs)
    cols = [slice(p * LANES, (p + 1) * LANES) for p in pairs]

    r = [load(r_ref, cs) for cs in cols]
    v = [load(v_ref, cs) for cs in cols]
    lw = [load(lw_ref, cs) for cs in cols]
    c3 = [_dot(tril, jnp.concatenate(_split3(x), axis=1)) for x in lw]
    c = [x[:, :LANES] + x[:, LANES:2 * LANES] + x[:, 2 * LANES:] for x in c3]

    k, a, b = [], [], []
    for p in pairs:
        k_raw = load(k_ref, cols[p])
        al = load(al_ref, cols[p])
        kk = k_raw * kk_ref[:, cols[p]]
        kk = kk / jnp.maximum(jnp.sqrt(seg_sum(kk * kk)), 1e-12)
        k.append(k_raw * (1.0 + (al - 1.0) * ka_ref[:, cols[p]]))
        a.append(-kk)
        b.append(kk * al)

    at2, v2, rt, bc2, kc2, d_end, gram = [], [], [], [], [], [], []
    for p in pairs:
        c_last = c[p][L - 1:L]
        e_neg = jnp.exp(-c[p])
        e_end = jnp.exp(c_last - c[p])
        at2.append(stack2(a[p] * jnp.exp(c[p] - lw[p])))
        rt.append(r[p] * jnp.exp(c[p]))
        v2.append(stack2(v[p]))
        bc2.append(stack2(b[p] * e_end))
        kc2.append(stack2(k[p] * e_end))
        d_end.append(jnp.exp(c_last))
        gram.append(_dot_nt(
            jnp.concatenate([at2[p], stack2(rt[p])], axis=0).astype(BF16),
            jnp.concatenate([stack2(b[p] * e_neg), stack2(k[p] * e_neg)], axis=0).astype(BF16)))

    n_mat = [jnp.where(strict_lower, g[:L2, :L2], 0.0) for g in gram]
    akv2 = [_dot(jnp.where(strict_lower, gram[p][:L2, L2:], 0.0).astype(BF16), v2[p].astype(BF16))
            for p in pairs]
    lhs_y = [jnp.concatenate([fold(jnp.where(lower, g[L2:, :L2], 0.0)),
                              fold(jnp.where(lower, g[L2:, L2:], 0.0))], axis=1) for g in gram]

    size = 1
    t_inv = [eye2 for _ in pairs]
    while size < L:
        quadrant = ((row2 // (2 * size)) == (col2 // (2 * size))) & \
                   ((row2 // size) % 2 == 1) & ((col2 // size) % 2 == 0)
        n_c = [jnp.where(quadrant, n, 0.0) for n in n_mat]
        if size == 1:
            t_inv = [t + n for t, n in zip(t_inv, n_c)]
        else:
            t_bf = [t.astype(BF16) for t in t_inv]
            tn = [_dot(t, n.astype(BF16)).astype(BF16) for t, n in zip(t_bf, n_c)]
            t_inv = [t + _dot(x, tb) for t, x, tb in zip(t_inv, tn, t_bf)]
        size *= 2

    u2 = [_dot(t_inv[p].astype(BF16), jnp.concatenate([at2[p], akv2[p]], axis=1).astype(BF16))
          for p in pairs]
    yz = []
    for p in pairs:
        z = jnp.concatenate(
            [u2[p], jnp.concatenate([jnp.zeros((L2, LANES), F32), v2[p]], axis=1)], axis=0).astype(BF16)
        lhs_h = jnp.concatenate([bc2[p], kc2[p]], axis=0).T
        yz.append(_dot(jnp.concatenate([lhs_y[p], lhs_h], axis=0).astype(BF16), z))

    hh = []
    for p in pairs:
        q_hat = rt[p] + yz[p][:L, :LANES]
        m_mat = eye_k * d_end[p] + yz[p][L:, :LANES]
        hh.append(_dot(jnp.concatenate([m_mat, q_hat], axis=0).astype(BF16), h_ref[p].astype(BF16)))

    for p in pairs:
        h_ref[p] = jnp.where(same_head, hh[p][:LANES] + yz[p][L:, LANES:], 0.0)
        y = hh[p][LANES:] + yz[p][:L, LANES:]
        mean = seg_sum(y) * (1.0 / half)
        yc = y - mean
        var = seg_sum(yc * yc) * (1.0 / half)
        yn = yc * lax.rsqrt(var + GN_EPS) * lnw_ref[:, cols[p]] + lnb_ref[:, cols[p]]
        bonus = seg_sum(r[p] * k[p] * rk_ref[:, cols[p]]) * v[p]
        out = yn + bonus
        o_ref[:, cols[p]] = (out[:valid] * gate_ref[:, cols[p]]).astype(o_ref.dtype)

    @pl.when(c_idx == pl.num_programs(1) - 1)
    def _():
        for p in range(n_pairs):
            h_last = h_ref[p]
            s_out_ref[2 * p] = h_last[:half, :half].T
            s_out_ref[2 * p + 1] = h_last[half:, half:].T


def _wkv(rkv, lw, al, gate, k_k, k_a, r_k, ln_w, ln_b, state, *, n_streams, n_chunks, rows, first_block):
    d = lw.shape[1]
    n_pairs = d // LANES

    def block(b, c):
        return first_block + b * n_chunks + c

    seq = pl.BlockSpec((rows, d), lambda b, c: (block(b, c), 0))
    rkv_specs = [pl.BlockSpec((None, rows, d), lambda b, c, s=s: (s, block(b, c), 0)) for s in range(3)]
    vec = pl.BlockSpec((1, d), lambda b, c: (0, 0))
    st = pl.BlockSpec((None,) + state.shape[1:], lambda b, c: (b, 0, 0, 0))
    return pl.pallas_call(
        _wkv_kernel,
        grid=(n_streams, n_chunks),
        in_specs=rkv_specs + [seq] * 3 + [vec] * 5 + [st],
        out_specs=[pl.BlockSpec((rows, d), lambda b, c: (b * n_chunks + c, 0)), st],
        out_shape=[jax.ShapeDtypeStruct((n_streams * n_chunks * rows, d), BF16),
                   jax.ShapeDtypeStruct(state.shape, F32)],
        scratch_shapes=[pltpu.VMEM((n_pairs, LANES, LANES), F32)],
        compiler_params=_params("parallel", "arbitrary"),
        name="rwkv7_wkv",
    )(rkv, rkv, rkv, lw, al, gate, *[x.reshape(1, d) for x in (k_k, k_a, r_k, ln_w, ln_b)], state)


def _pad_lora(w_in, w_out):
    rank = w_in.shape[1]
    pad = (-rank) % LORA_PAD
    return (jnp.pad(w_in, ((0, 0), (0, pad))).astype(BF16),
            jnp.pad(w_out, ((0, pad), (0, 0))).astype(BF16))


def kernel(x_prompt, x_sample, cache_k, cache_v, state_shift, state_wkv, norm_g, ffn_w_gate, ffn_w_up, ffn_w_down, rel_table, att_w_qkv, att_b_qkv, att_w_o, att_sinks, rwkv_mu, rwkv_w_r, rwkv_w_k, rwkv_w_v, rwkv_w_o, rwkv_w0, rwkv_w1, rwkv_w2, rwkv_a0, rwkv_a1, rwkv_a2, rwkv_g1, rwkv_g2, rwkv_k_k, rwkv_k_a, rwkv_r_k, rwkv_ln_w, rwkv_ln_b):
    n_bp, t_p, d = x_prompt.shape
    n_bs, t_s, _ = x_sample.shape
    n_p = n_bp * t_p
    n_s = n_bs * t_s
    depth = norm_g.shape[0]
    q_dim = att_w_o.shape[1]
    kv_dim = N_KV_HEADS * HEAD_DIM
    assert n_p % ROW_TILE == 0 and n_s % ROW_TILE == 0 and ROW_TILE % t_s == 0 and t_p % CHUNK == 0
    assert t_p % WKV_CHUNK == 0 and t_s <= WKV_CHUNK and t_p % t_s == 0
    assert ROW_TILE % RWKV_ROW_TILE == 0 and RWKV_ROW_TILE % t_s == 0

    ffn_w = (_tile_gate_up(ffn_w_gate, ffn_w_up), ffn_w_down.astype(BF16))

    x = (x_prompt.reshape(n_p, d), x_sample.reshape(n_s, d))
    new_k, new_v, new_shift, new_wkv = [], [], [], []
    for i in range(depth):
        g = norm_g[i]
        j = i // 2
        x = _ffn_block(x, g[0], g[1], ffn_w, i, 0)
        if i % 2 == 0:
            q, k, v = _qkv_proj(x, g[2], att_w_qkv[j].astype(BF16), att_b_qkv[j], q_dim, kv_dim)

            n_back = WINDOW // CHUNK
            n_c = t_p // CHUNK
            parts = [(k, v, CHUNK, (lambda b, c, back=back: (b * n_c + jnp.maximum(c - back, 0), 0)), back)
                     for back in range(n_back, -1, -1)]
            o_p = _attention(q, parts, _logit_offsets(rel_table, att_sinks[j], CHUNK, (n_back + 1) * CHUNK),
                             grid=(n_bp, n_c), q_rows=CHUNK, q_map=lambda b, c: (b * n_c + c, 0),
                             n_out_rows=n_p)
            first = n_p // t_s
            parts = [(cache_k[j].reshape(n_bs, WINDOW, kv_dim), cache_v[j].reshape(n_bs, WINDOW, kv_dim),
                      WINDOW, (lambda b, c: (b, 0, 0)), 0),
                     (k, v, t_s, (lambda b, c: (first + b, 0)), 0)]
            o_s = _attention(q, parts, _logit_offsets(rel_table, att_sinks[j], t_s, WINDOW + t_s),
                             grid=(n_bs, 1), q_rows=t_s, q_map=lambda b, c: (first + b, 0),
                             n_out_rows=n_s)
            x = _out_proj(o_p, o_s, att_w_o[j].astype(BF16), g[3], x)

            def kv_rows(a):
                newest = jnp.stack([a[(b + 1) * t_p - WINDOW:(b + 1) * t_p] for b in range(n_bp)])
                return (newest.reshape(n_bp, WINDOW, N_KV_HEADS, HEAD_DIM),
                        a[n_p:].reshape(n_bs, t_s, N_KV_HEADS, HEAD_DIM))

            new_k.append(kv_rows(k))
            new_v.append(kv_rows(v))
        else:
            first_rows = jnp.broadcast_to(state_shift[j], (n_bs, t_s, d)).reshape(n_s, d)
            streams = (n_p, t_p, t_s)
            mu = rwkv_mu[j]
            rkv = _rkv_proj(x, g[2], first_rows, streams, jnp.stack([mu[0], mu[2], mu[3]]),
                            jnp.stack([rwkv_w_r[j], rwkv_w_k[j], rwkv_w_v[j]]).astype(BF16))
            w1, w2 = _pad_lora(rwkv_w1[j], rwkv_w2[j])
            a1, a2 = _pad_lora(rwkv_a1[j], rwkv_a2[j])
            g1, g2 = _pad_lora(rwkv_g1[j], rwkv_g2[j])
            lw, al, gate, h_tail = _lora_branches(x, g[2], first_rows, streams,
                                                  jnp.stack([mu[1], mu[4], mu[5]]), rwkv_w0[j], w1, w2,
                                                  rwkv_a0[j], a1, a2, g1, g2)
            vecs = (rwkv_k_k[j], rwkv_k_a[j], rwkv_r_k[j].reshape(d), rwkv_ln_w[j], rwkv_ln_b[j])
            zero_state = jnp.zeros((n_bp, d // RWKV_HEAD, RWKV_HEAD, RWKV_HEAD), F32)
            o_p, wkv_p = _wkv(rkv, lw, al, gate, *vecs, zero_state, n_streams=n_bp,
                              n_chunks=t_p // WKV_CHUNK, rows=WKV_CHUNK, first_block=0)
            o_s, wkv_s = _wkv(rkv, lw, al, gate, *vecs, state_wkv[j], n_streams=n_bs,
                              n_chunks=1, rows=t_s, first_block=n_p // t_s)
            x = _out_proj(o_p, o_s, rwkv_w_o[j].astype(BF16), g[3], x)
            groups = t_p // t_s
            new_shift.append((h_tail[groups - 1:n_p // t_s:groups].reshape(n_bp, 1, d),
                              h_tail[n_p // t_s:].reshape(n_bs, 1, d)))
            new_wkv.append((wkv_p, wkv_s))
        x = _ffn_block(x, g[4], g[5], ffn_w, i, 1, split_rows=n_p if i == depth - 1 else None)

    def both(pairs):
        return jnp.stack([p[0] for p in pairs]), jnp.stack([p[1] for p in pairs])

    k_prompt, k_sample = both(new_k)
    v_prompt, v_sample = both(new_v)
    shift_prompt, shift_sample = both(new_shift)
    wkv_prompt, wkv_sample = both(new_wkv)
    return (x[0].reshape(n_bp, t_p, d), x[1].reshape(n_bs, t_s, d),
            k_prompt, v_prompt, k_sample, v_sample,
            shift_prompt, wkv_prompt, shift_sample, wkv_sample)
```

```python
import functools
import math

import jax
import jax.numpy as jnp
from jax import lax
from jax.experimental import pallas as pl
from jax.experimental.pallas import tpu as pltpu

F32 = jnp.float32
BF16 = jnp.bfloat16

HEAD_DIM = 64
N_KV_HEADS = 4
CHUNK = 64
WINDOW = 128
N_BUCKETS = 32
MAX_DISTANCE = 128
RWKV_HEAD = 64
RMS_EPS = 1e-6
GN_EPS = RWKV_HEAD * 1e-5
NEG_INF = -1e30
LORA_PAD = 128

LANES = 128
KEY_PAD = 256
VMEM_LIMIT_BYTES = 56 * 1024 * 1024
ROW_TILE = 512
RWKV_ROW_TILE = 256
FF_TILE = 512
WKV_CHUNK = 64


def _params(*semantics):
    return pltpu.CompilerParams(dimension_semantics=semantics,
                                vmem_limit_bytes=VMEM_LIMIT_BYTES)


def _rms(x):
    return x * lax.rsqrt(jnp.mean(jnp.square(x), axis=-1, keepdims=True) + RMS_EPS)


def _dot(a, b):
    return jnp.dot(a, b, preferred_element_type=F32)


def _dot_nt(a, b):
    return lax.dot_general(a, b, (((1,), (1,)), ((), ())), preferred_element_type=F32)


def _two_part_specs(d, prompt_tiles):
    return [pl.BlockSpec((ROW_TILE, d), lambda i, *_: (jnp.minimum(i, prompt_tiles - 1), 0)),
            pl.BlockSpec((ROW_TILE, d), lambda i, *_: (jnp.maximum(i - prompt_tiles, 0), 0))]


def _ffn_kernel(*refs, n_x, n_out, prompt_tiles):
    x_refs = refs[:n_x]
    g0_ref, g1_ref, wg_ref, wu_ref, wd_ref = refs[n_x:n_x + 5]
    o_refs = refs[n_x + 5:n_x + 5 + n_out]
    h_ref = refs[n_x + 5 + n_out]
    i = pl.program_id(0)
    f = pl.program_id(1)

    def partial_down(h):
        a = _dot(h, wg_ref[...])
        b = _dot(h, wu_ref[...])
        return _dot((a * jax.nn.sigmoid(a) * b).astype(BF16), wd_ref[...])

    def body(x_ref, o_ref):
        last = pl.num_programs(1) - 1

        @pl.when(f == 0)
        def _():
            h = (_rms(x_ref[...]) * g0_ref[...]).astype(BF16)
            h_ref[...] = h
            o_ref[...] = partial_down(h)

        @pl.when((f > 0) & (f < last))
        def _():
            o_ref[...] += partial_down(h_ref[...])

        @pl.when(f == last)
        def _():
            y = o_ref[...] + partial_down(h_ref[...])
            o_ref[...] = x_ref[...] + 0.5 * (_rms(y) * g1_ref[...])

    if n_x == 1 and n_out == 1:
        body(x_refs[0], o_refs[0])
    else:
        pl.when(i < prompt_tiles)(lambda: body(x_refs[0], o_refs[0]))
        pl.when(i >= prompt_tiles)(lambda: body(x_refs[-1], o_refs[-1]))


def _ffn_block(x, g0, g1, w, layer, half, split_rows=None):
    xs = x if isinstance(x, tuple) else (x,)
    d = xs[0].shape[1]
    n = sum(a.shape[0] for a in xs)
    d_ff = w[0].shape[-1]
    assert d_ff // FF_TILE >= 2
    row = pl.BlockSpec((ROW_TILE, d), lambda i, f: (i, 0))
    prompt_tiles = n // ROW_TILE
    x_specs, out_specs, out_shape = [row], [row], [jax.ShapeDtypeStruct((n, d), F32)]
    if len(xs) == 2:
        prompt_tiles = xs[0].shape[0] // ROW_TILE
        x_specs = _two_part_specs(d, prompt_tiles)
    if split_rows is not None:
        prompt_tiles = split_rows // ROW_TILE
        out_specs = _two_part_specs(d, prompt_tiles)
        out_shape = [jax.ShapeDtypeStruct((split_rows, d), F32),
                     jax.ShapeDtypeStruct((n - split_rows, d), F32)]
    out = pl.pallas_call(
        functools.partial(_ffn_kernel, n_x=len(xs), n_out=len(out_shape), prompt_tiles=prompt_tiles),
        grid=(n // ROW_TILE, d_ff // FF_TILE),
        in_specs=x_specs + [
            pl.BlockSpec((1, d), lambda i, f: (0, 0)),
            pl.BlockSpec((1, d), lambda i, f: (0, 0)),
            pl.BlockSpec((None, None, d, FF_TILE), lambda i, f: (layer, half, 0, f)),
            pl.BlockSpec((None, None, d, FF_TILE), lambda i, f: (layer, half, 0, f)),
            pl.BlockSpec((None, None, FF_TILE, d), lambda i, f: (layer, half, f, 0)),
        ],
        out_specs=out_specs,
        out_shape=out_shape,
        scratch_shapes=[pltpu.VMEM((ROW_TILE, d), BF16)],
        compiler_params=_params("arbitrary", "arbitrary"),
        name="ffn_block",
    )(*xs, g0.reshape(1, d), g1.reshape(1, d), *w)
    return tuple(out) if split_rows is not None else out[0]


def _normed_and_shifted(x_ref, g_ref, first_ref, carry_ref, streams):
    prompt_tiles, prompt_len, sample_len = streams
    tile = x_ref.shape[0]
    i = pl.program_id(0)

    @pl.when(i == 0)
    def _():
        carry_ref[...] = jnp.zeros_like(carry_ref)

    h = _rms(x_ref[...]) * g_ref[...]
    rows = lax.broadcasted_iota(jnp.int32, (tile, 1), 0)
    prev = jnp.where(rows == 0, carry_ref[...], pltpu.roll(h, 1, axis=0))
    carry_ref[...] = h[tile - 1:tile]
    prompt_start = (rows + i * tile) % prompt_len == 0
    sample_start = rows % sample_len == 0
    return h, jnp.where(i < prompt_tiles, jnp.where(prompt_start, 0.0, prev),
                        jnp.where(sample_start, first_ref[...], prev))


def _shift_specs(d, tile, n_prompt_rows):
    prompt_tiles = n_prompt_rows // tile
    return [pl.BlockSpec((tile, d), lambda i: (i, 0)),
            pl.BlockSpec((1, d), lambda i: (0, 0)),
            pl.BlockSpec((tile, d), lambda i: (jnp.maximum(i - prompt_tiles, 0), 0))]


def _qkv_kernel(x_ref, g_ref, w_ref, b_ref, q_ref, k_ref, v_ref):
    h = (_rms(x_ref[...]) * g_ref[...]).astype(BF16)
    y = _dot(h, w_ref[...]) + b_ref[...]
    q_dim = q_ref.shape[1]
    kv_dim = k_ref.shape[1]
    q_ref[...] = y[:, :q_dim].astype(q_ref.dtype)
    k_ref[...] = y[:, q_dim:q_dim + kv_dim]
    v_ref[...] = y[:, q_dim + kv_dim:]


def _qkv_proj(x, g, w, b, q_dim, kv_dim):
    n, d = x.shape
    m = w.shape[1]

    def rows(width):
        return pl.BlockSpec((ROW_TILE, width), lambda i: (i, 0))

    return pl.pallas_call(
        _qkv_kernel,
        grid=(n // ROW_TILE,),
        in_specs=[rows(d), pl.BlockSpec((1, d), lambda i: (0, 0)),
                  pl.BlockSpec((d, m), lambda i: (0, 0)), pl.BlockSpec((1, m), lambda i: (0, 0))],
        out_specs=[rows(q_dim), rows(kv_dim), rows(kv_dim)],
        out_shape=[jax.ShapeDtypeStruct((n, q_dim), BF16), jax.ShapeDtypeStruct((n, kv_dim), F32),
                   jax.ShapeDtypeStruct((n, kv_dim), F32)],
        compiler_params=_params("parallel"),
        name="qkv_proj",
    )(x, g.reshape(1, d), w, b.reshape(1, m))


def _rkv_kernel(x_ref, g_ref, first_ref, mu_ref, w_ref, o_ref, carry_ref, *, streams):
    h, hp = _normed_and_shifted(x_ref, g_ref, first_ref, carry_ref, streams)
    dx = hp - h
    for s in range(w_ref.shape[0]):
        o_ref[s] = _dot((h + dx * mu_ref[s]).astype(BF16), w_ref[s])


def _rkv_proj(x, g, first_rows, streams, mu, w):
    n, d = x.shape
    k = w.shape[0]
    m = w.shape[2]
    n_prompt_rows, prompt_len, sample_len = streams
    return pl.pallas_call(
        functools.partial(_rkv_kernel, streams=(n_prompt_rows // RWKV_ROW_TILE, prompt_len, sample_len)),
        grid=(n // RWKV_ROW_TILE,),
        in_specs=_shift_specs(d, RWKV_ROW_TILE, n_prompt_rows) + [
            pl.BlockSpec((k, 1, d), lambda i: (0, 0, 0)),
            pl.BlockSpec((k, d, m), lambda i: (0, 0, 0), pipeline_mode=pl.Buffered(1))],
        out_specs=pl.BlockSpec((k, RWKV_ROW_TILE, m), lambda i: (0, i, 0)),
        out_shape=jax.ShapeDtypeStruct((k, n, m), F32),
        scratch_shapes=[pltpu.VMEM((1, d), F32)],
        compiler_params=_params("arbitrary"),
        name="rkv_proj",
    )(x, g.reshape(1, d), first_rows, mu.reshape(k, 1, d), w)


def _lora_kernel(x_ref, g_ref, first_ref, mu_ref, w0_ref, w1_ref, w2_ref, a0_ref, a1_ref, a2_ref,
                 g1_ref, g2_ref, lw_ref, al_ref, gate_ref, tail_ref, carry_ref, *, streams):
    h, hp = _normed_and_shifted(x_ref, g_ref, first_ref, carry_ref, streams)
    dx = hp - h
    sample_len = streams[2]
    tail_ref[...] = jnp.concatenate(
        [h[(t + 1) * sample_len - 1:(t + 1) * sample_len] for t in range(tail_ref.shape[0])], axis=0)

    xw = (h + dx * mu_ref[0]).astype(BF16)
    t = jnp.tanh(_dot(xw, w1_ref[...])).astype(BF16)
    u_w = w0_ref[...] + _dot(t, w2_ref[...])
    lw_ref[...] = -math.exp(-0.5) * jax.nn.sigmoid(u_w)

    xa = (h + dx * mu_ref[1]).astype(BF16)
    u = _dot(xa, a1_ref[...]).astype(BF16)
    al_ref[...] = jax.nn.sigmoid(a0_ref[...] + _dot(u, a2_ref[...]))

    xg = (h + dx * mu_ref[2]).astype(BF16)
    s = jax.nn.sigmoid(_dot(xg, g1_ref[...])).astype(BF16)
    gate_ref[...] = _dot(s, g2_ref[...])


def _lora_branches(x, g, first_rows, streams, mu, w0, w1, w2, a0, a1, a2, g1, g2):
    n, d = x.shape
    n_prompt_rows, prompt_len, sample_len = streams
    tile = RWKV_ROW_TILE
    row = pl.BlockSpec((tile, d), lambda i: (i, 0))
    tail = pl.BlockSpec((tile // sample_len, d), lambda i: (i, 0))

    def whole(a):
        return pl.BlockSpec(a.shape, lambda i: (0,) * a.ndim)

    mu = mu.reshape(3, 1, d)
    w0 = w0.reshape(1, d)
    a0 = a0.reshape(1, d)
    consts = (mu, w0, w1, w2, a0, a1, a2, g1, g2)
    out = jax.ShapeDtypeStruct((n, d), F32)
    return pl.pallas_call(
        functools.partial(_lora_kernel, streams=(n_prompt_rows // tile, prompt_len, sample_len)),
        grid=(n // tile,),
        in_specs=_shift_specs(d, tile, n_prompt_rows) + [whole(c) for c in consts],
        out_specs=[row, row, row, tail],
        out_shape=[out, out, out, jax.ShapeDtypeStruct((n // sample_len, d), F32)],
        scratch_shapes=[pltpu.VMEM((1, d), F32)],
        compiler_params=_params("arbitrary"),
        name="rwkv_lora",
    )(x, g.reshape(1, d), first_rows, *consts)


def _out_proj_kernel(op_ref, os_ref, w_ref, g_ref, x_ref, y_ref, *, prompt_tiles):
    o = jnp.where(pl.program_id(0) < prompt_tiles, op_ref[...], os_ref[...])
    y = _dot(o, w_ref[...])
    y_ref[...] = x_ref[...] + _rms(y) * g_ref[...]


def _out_proj(o_prompt, o_sample, w, g, xres):
    n, d = xres.shape
    k = o_prompt.shape[1]
    prompt_tiles = o_prompt.shape[0] // ROW_TILE
    return pl.pallas_call(
        functools.partial(_out_proj_kernel, prompt_tiles=prompt_tiles),
        grid=(n // ROW_TILE,),
        in_specs=_two_part_specs(k, prompt_tiles) + [
            pl.BlockSpec((k, d), lambda i: (0, 0)),
            pl.BlockSpec((1, d), lambda i: (0, 0)),
            pl.BlockSpec((ROW_TILE, d), lambda i: (i, 0))],
        out_specs=pl.BlockSpec((ROW_TILE, d), lambda i: (i, 0)),
        out_shape=jax.ShapeDtypeStruct((n, d), F32),
        compiler_params=_params("arbitrary"),
        name="out_proj",
    )(o_prompt, o_sample, w, g.reshape(1, d), xres)


def _t5_bucket(rel):
    nb = N_BUCKETS // 2
    max_exact = nb // 2
    offset = jnp.where(rel > 0, nb, 0)
    n = jnp.abs(rel)
    nf = jnp.maximum(n, 1).astype(F32)
    large = max_exact + (jnp.log(nf / max_exact) / math.log(MAX_DISTANCE / max_exact)
                         * (nb - max_exact)).astype(jnp.int32)
    large = jnp.minimum(large, nb - 1)
    return offset + jnp.where(n < max_exact, n, large)


def _bias_kernel(tt_ref, b_ref, o_ref):
    onehot = (lax.broadcasted_iota(jnp.int32, (tt_ref.shape[1], b_ref.shape[1]), 0) == b_ref[...]).astype(F32)
    o_ref[...] = jnp.dot(tt_ref[...], onehot, preferred_element_type=F32,
                         precision=lax.Precision.HIGHEST)


def _logit_offsets(table, sinks, n_q, n_k):
    assert n_k < KEY_PAD
    rel = (jnp.arange(n_k, dtype=jnp.int32)[None, :] - WINDOW - jnp.arange(n_q, dtype=jnp.int32)[:, None])
    rows = jnp.concatenate([_t5_bucket(rel).astype(jnp.int32),
                            jnp.full((n_q, 1), N_BUCKETS, jnp.int32),
                            jnp.full((n_q, KEY_PAD - n_k - 1), N_BUCKETS + 1, jnp.int32)], axis=1)
    n_heads = table.shape[1]
    values = jnp.concatenate([table.T, sinks.astype(F32)[:, None],
                              jnp.full((n_heads, 1), NEG_INF, F32)], axis=1)
    out = pl.pallas_call(
        _bias_kernel,
        out_shape=jax.ShapeDtypeStruct((n_heads, n_q * KEY_PAD), F32),
        name="logit_offsets",
    )(values, rows.reshape(1, n_q * KEY_PAD))
    return out.reshape(n_heads, n_q, KEY_PAD)


def _attn_kernel(*refs, n_parts, group, first_valid_step):
    q_ref = refs[0]
    k_refs = refs[1:1 + n_parts]
    v_refs = refs[1 + n_parts:1 + 2 * n_parts]
    off_ref, o_ref = refs[1 + 2 * n_parts:]
    n_q = q_ref.shape[0]
    step = pl.program_id(1)
    scale = HEAD_DIM ** -0.5
    part_rows = [r.shape[0] for r in k_refs]
    n_k = sum(part_rows)
    heads = range(N_KV_HEADS)
    tiles = range(group // 2)

    col = lax.broadcasted_iota(jnp.int32, (1, KEY_PAD), 1)
    hidden = None
    start = 0
    for p, rows in enumerate(part_rows):
        if first_valid_step[p] > 0:
            h_p = (col >= start) & (col < start + rows) & (step < first_valid_step[p])
            hidden = h_p if hidden is None else hidden | h_p
        start += rows

    zeros = jnp.zeros((KEY_PAD, HEAD_DIM), F32)
    ones = jnp.ones((KEY_PAD, HEAD_DIM), F32)

    def head_rows(refs_, hk):
        kv_cols = slice(hk * HEAD_DIM, (hk + 1) * HEAD_DIM)
        return jnp.concatenate([r[:, kv_cols] for r in refs_] + [zeros[:KEY_PAD - n_k]], axis=0)

    scores = []
    for hk in heads:
        k = head_rows(k_refs, hk)
        k2 = jnp.concatenate([jnp.concatenate([k, zeros], axis=1),
                              jnp.concatenate([zeros, k], axis=1)], axis=0).astype(BF16)
        q_rows = jnp.concatenate(
            [q_ref[:, (hk * group + 2 * t) * HEAD_DIM:(hk * group + 2 * t + 2) * HEAD_DIM] for t in tiles],
            axis=0)
        off = jnp.concatenate(
            [jnp.concatenate([off_ref[hk * group + 2 * t], off_ref[hk * group + 2 * t + 1]], axis=1)
             for t in tiles], axis=0)
        s = _dot_nt(q_rows, k2) * scale + off
        if hidden is not None:
            s = jnp.where(jnp.concatenate([hidden, hidden], axis=1), NEG_INF, s)
        scores.append(s)

    expo = []
    for hk in heads:
        s = scores[hk]
        m0 = jnp.max(s[:, :KEY_PAD], axis=-1, keepdims=True)
        m1 = jnp.max(s[:, KEY_PAD:], axis=-1, keepdims=True)
        expo.append(jnp.concatenate([jnp.exp(s[:, :KEY_PAD] - m0), jnp.exp(s[:, KEY_PAD:] - m1)],
                                    axis=1).astype(BF16))

    for hk in heads:
        v = head_rows(v_refs, hk)
        rhs = jnp.concatenate([jnp.concatenate([v, zeros, ones, zeros], axis=1),
                               jnp.concatenate([zeros, v, zeros, ones], axis=1)], axis=0).astype(BF16)
        acc = _dot(expo[hk], rhs)
        o = acc[:, :LANES] / acc[:, LANES:]
        for t in tiles:
            first_head = hk * group + 2 * t
            o_ref[:, first_head * HEAD_DIM:(first_head + 2) * HEAD_DIM] = (
                o[t * n_q:(t + 1) * n_q].astype(o_ref.dtype))


def _attention(q, parts, offsets, *, grid, q_rows, q_map, n_out_rows):
    q_dim = q.shape[1]
    n_heads = q_dim // HEAD_DIM
    group = n_heads // N_KV_HEADS
    assert offsets.shape == (n_heads, q_rows, KEY_PAD) and group % 2 == 0

    def kv_spec(arr, rows, imap):
        if arr.ndim == 3:
            return pl.BlockSpec((None, rows, arr.shape[2]), imap)
        return pl.BlockSpec((rows, arr.shape[1]), imap)

    k_specs = [kv_spec(p[0], p[2], p[3]) for p in parts]
    v_specs = [kv_spec(p[1], p[2], p[3]) for p in parts]
    n_steps = grid[1]
    return pl.pallas_call(
        functools.partial(_attn_kernel, n_parts=len(parts), group=group,
                          first_valid_step=tuple(p[4] for p in parts)),
        grid=grid,
        in_specs=[pl.BlockSpec((q_rows, q_dim), q_map)] + k_specs + v_specs + [
            pl.BlockSpec(offsets.shape, lambda b, c: (0, 0, 0))],
        out_specs=pl.BlockSpec((q_rows, q_dim), lambda b, c: (b * n_steps + c, 0)),
        out_shape=jax.ShapeDtypeStruct((n_out_rows, q_dim), BF16),
        compiler_params=_params("parallel", "arbitrary"),
        name="swa_attention",
    )(q, *[p[0] for p in parts], *[p[1] for p in parts], offsets)


def _split3(x):
    hi = x.astype(BF16)
    r1 = x - hi.astype(F32)
    mid = r1.astype(BF16)
    lo = (r1 - mid.astype(F32)).astype(BF16)
    return hi, mid, lo


def _wkv_kernel(r_ref, k_ref, v_ref, lw_ref, al_ref, gate_ref, kk_ref, ka_ref, rk_ref, lnw_ref,
                lnb_ref, s0_ref, o_ref, s_out_ref, h_ref):
    valid = r_ref.shape[0]
    L = WKV_CHUNK
    while L // 2 >= max(valid, 16):
        L //= 2
    L2 = 2 * L
    n_pairs = r_ref.shape[1] // LANES
    c_idx = pl.program_id(1)
    half = RWKV_HEAD

    lane = lax.broadcasted_iota(jnp.int32, (1, LANES), 1)
    head0 = lane < half
    row2 = lax.broadcasted_iota(jnp.int32, (L2, L2), 0)
    col2 = lax.broadcasted_iota(jnp.int32, (L2, L2), 1)
    strict_lower = row2 > col2
    lower = row2 >= col2
    eye2 = (row2 == col2).astype(F32)
    tril = (lax.broadcasted_iota(jnp.int32, (L, L), 0) >= lax.broadcasted_iota(jnp.int32, (L, L), 1)
            ).astype(BF16)
    krow = lax.broadcasted_iota(jnp.int32, (LANES, LANES), 0)
    kcol = lax.broadcasted_iota(jnp.int32, (LANES, LANES), 1)
    same_head = (krow < half) == (kcol < half)
    eye_k = (krow == kcol).astype(F32)

    def load(ref, cs):
        x = ref[:, cs]
        if valid < L:
            x = jnp.concatenate([x, jnp.zeros((L - valid, LANES), x.dtype)], axis=0)
        return x

    def seg_sum(x):
        s0 = jnp.sum(jnp.where(head0, x, 0.0), axis=-1, keepdims=True)
        s1 = jnp.sum(jnp.where(head0, 0.0, x), axis=-1, keepdims=True)
        return jnp.where(head0, s0, s1)

    def stack2(x):
        return jnp.concatenate([jnp.where(head0, x, 0.0), jnp.where(head0, 0.0, x)], axis=0)

    def fold(m):
        return m[:L] + m[L:]

    @pl.when(c_idx == 0)
    def _():
        for p in range(n_pairs):
            s0 = jnp.concatenate([s0_ref[2 * p].T, s0_ref[2 * p + 1].T], axis=0)
            h_ref[p] = jnp.where(same_head, jnp.concatenate([s0, s0], axis=1), 0.0)

    cols =[slice(p * LANES, (p + 1) * LANES) for p in range(n_pairs)]
    val = [dict() for _ in range(n_pairs)]

    def s_load(ps):
        for p in ps:
            t = val[p]
            t["r"], t["v"], t["lw"] = load(r_ref, cols[p]), load(v_ref, cols[p]), load(lw_ref, cols[p])
            c3 = _dot(tril, jnp.concatenate(_split3(t["lw"]), axis=1))
            t["c"] = c3[:, :LANES] + c3[:, LANES:2 * LANES] + c3[:, 2 * LANES:]

    def s_keys(ps):
        for p in ps:
            t = val[p]
            k_raw = load(k_ref, cols[p])
            al = load(al_ref, cols[p])
            kk = k_raw * kk_ref[:, cols[p]]
            kk = kk / jnp.maximum(jnp.sqrt(seg_sum(kk * kk)), 1e-12)
            t["k"] = k_raw * (1.0 + (al - 1.0) * ka_ref[:, cols[p]])
            t["a"] = -kk
            t["b"] = kk * al

    def s_gram(ps):
        for p in ps:
            t = val[p]
            c = t["c"]
            c_last = c[L - 1:L]
            e_neg = jnp.exp(-c)
            e_end = jnp.exp(c_last - c)
            t["at2"] = stack2(t["a"] * jnp.exp(c - t["lw"]))
            t["rt"] = t["r"] * jnp.exp(c)
            t["v2"] = stack2(t["v"])
            t["bc2"] = stack2(t["b"] * e_end)
            t["kc2"] = stack2(t["k"] * e_end)
            t["d_end"] = jnp.exp(c_last)
            t["gram"] = _dot_nt(
                jnp.concatenate([t["at2"], stack2(t["rt"])], axis=0).astype(BF16),
                jnp.concatenate([stack2(t["b"] * e_neg), stack2(t["k"] * e_neg)], axis=0).astype(BF16))

    def s_masks(ps):
        for p in ps:
            t = val[p]
            g = t.pop("gram")
            t["n_mat"] = jnp.where(strict_lower, g[:L2, :L2], 0.0)
            t["akv2"] = _dot(jnp.where(strict_lower, g[:L2, L2:], 0.0).astype(BF16), t["v2"].astype(BF16))
            t["lhs_y"] = jnp.concatenate([fold(jnp.where(lower, g[L2:, :L2], 0.0)),
                                          fold(jnp.where(lower, g[L2:, L2:], 0.0))], axis=1)
            quadrant = (row2 // 2 == col2 // 2) & (row2 % 2 == 1) & (col2 % 2 == 0)
            t["t_inv"] = eye2 + jnp.where(quadrant, t["n_mat"], 0.0)

    def s_level(size):
        def stage(ps):
            quadrant = ((row2 // (2 * size)) == (col2 // (2 * size))) & \
                       ((row2 // size) % 2 == 1) & ((col2 // size) % 2 == 0)
            t_bf = {p: val[p]["t_inv"].astype(BF16) for p in ps}
            tn = {p: _dot(t_bf[p], jnp.where(quadrant, val[p]["n_mat"], 0.0).astype(BF16)).astype(BF16)
                  for p in ps}
            for p in ps:
                val[p]["t_inv"] = val[p]["t_inv"] + _dot(tn[p], t_bf[p])
        return stage

    def s_solve(ps):
        for p in ps:
            t = val[p]
            t["u2"] = _dot(t.pop("t_inv").astype(BF16),
                           jnp.concatenate([t["at2"], t.pop("akv2")], axis=1).astype(BF16))

    def s_yz(ps):
        for p in ps:
            t = val[p]
            z = jnp.concatenate(
                [t.pop("u2"), jnp.concatenate([jnp.zeros((L2, LANES), F32), t["v2"]], axis=1)],
                axis=0).astype(BF16)
            lhs_h = jnp.concatenate([t["bc2"], t["kc2"]], axis=0).T
            t["yz"] = _dot(jnp.concatenate([t["lhs_y"], lhs_h], axis=0).astype(BF16), z)

    def s_state(ps):
        for p in ps:
            t = val[p]
            q_hat = t["rt"] + t["yz"][:L, :LANES]
            m_mat = eye_k * t["d_end"] + t["yz"][L:, :LANES]
            t["hh"] = _dot(jnp.concatenate([m_mat, q_hat], axis=0).astype(BF16), h_ref[p].astype(BF16))

    def s_out(ps):
        for p in ps:
            t = val[p]
            h_ref[p] = jnp.where(same_head, t["hh"][:LANES] + t["yz"][L:, LANES:], 0.0)
            y = t["hh"][LANES:] + t["yz"][:L, LANES:]
            mean = seg_sum(y) * (1.0 / half)
            yc = y - mean
            var = seg_sum(yc * yc) * (1.0 / half)
            yn = yc * lax.rsqrt(var + GN_EPS) * lnw_ref[:, cols[p]] + lnb_ref[:, cols[p]]
            bonus = seg_sum(t["r"] * t["k"] * rk_ref[:, cols[p]]) * t["v"]
            out = yn + bonus
            o_ref[:, cols[p]] = (out[:valid] * gate_ref[:, cols[p]]).astype(o_ref.dtype)
            t.clear()

    sizes = []
    size = 2
    while size < L:
        sizes.append(size)
        size *= 2
    for stage in [s_load, s_keys, s_gram, s_masks] + [s_level(sz) for sz in sizes] + [s_solve, s_yz, s_state, s_out]:
        stage(range(n_pairs))

    @pl.when(c_idx == pl.num_programs(1) - 1)
    def _():
        for p in range(n_pairs):
            h_last = h_ref[p]
            s_out_ref[2 * p] = h_last[:half, :half].T
            s_out_ref[2 * p + 1] = h_last[half:, half:].T


def _wkv(rkv, lw, al, gate, k_k, k_a, r_k, ln_w, ln_b, state, *, n_streams, n_chunks, rows, first_block):
    d = lw.shape[1]
    n_pairs = d // LANES

    def block(b, c):
        return first_block + b * n_chunks + c

    seq = pl.BlockSpec((rows, d), lambda b, c: (block(b, c), 0))
    rkv_specs = [pl.BlockSpec((None, rows, d), lambda b, c, s=s: (s, block(b, c), 0)) for s in range(3)]
    vec = pl.BlockSpec((1, d), lambda b, c: (0, 0))
    st = pl.BlockSpec((None,) + state.shape[1:], lambda b, c: (b, 0, 0, 0))
    return pl.pallas_call(
        _wkv_kernel,
        grid=(n_streams, n_chunks),
        in_specs=rkv_specs + [seq] * 3 + [vec] * 5 + [st],
        out_specs=[pl.BlockSpec((rows, d), lambda b, c: (b * n_chunks + c, 0)), st],
        out_shape=[jax.ShapeDtypeStruct((n_streams * n_chunks * rows, d), BF16),
                   jax.ShapeDtypeStruct(state.shape, F32)],
        scratch_shapes=[pltpu.VMEM((n_pairs, LANES, LANES), F32)],
        compiler_params=_params("parallel", "arbitrary"),
        name="rwkv7_wkv",
    )(rkv, rkv, rkv, lw, al, gate, *[x.reshape(1, d) for x in (k_k, k_a, r_k, ln_w, ln_b)], state)


def _pad_lora(w_in, w_out):
    rank = w_in.shape[1]
    pad = (-rank) % LORA_PAD
    return (jnp.pad(w_in, ((0, 0), (0, pad))).astype(BF16),
            jnp.pad(w_out, ((0, pad), (0, 0))).astype(BF16))


def kernel(x_prompt, x_sample, cache_k, cache_v, state_shift, state_wkv, norm_g, ffn_w_gate, ffn_w_up, ffn_w_down, rel_table, att_w_qkv, att_b_qkv, att_w_o, att_sinks, rwkv_mu, rwkv_w_r, rwkv_w_k, rwkv_w_v, rwkv_w_o, rwkv_w0, rwkv_w1, rwkv_w2, rwkv_a0, rwkv_a1, rwkv_a2, rwkv_g1, rwkv_g2, rwkv_k_k, rwkv_k_a, rwkv_r_k, rwkv_ln_w, rwkv_ln_b):
    n_bp, t_p, d = x_prompt.shape
    n_bs, t_s, _ = x_sample.shape
    n_p = n_bp * t_p
    n_s = n_bs * t_s
    depth = norm_g.shape[0]
    q_dim = att_w_o.shape[1]
    kv_dim = N_KV_HEADS * HEAD_DIM
    assert n_p % ROW_TILE == 0 and n_s % ROW_TILE == 0 and ROW_TILE % t_s == 0 and t_p % CHUNK == 0
    assert t_p % WKV_CHUNK == 0 and t_s <= WKV_CHUNK and t_p % t_s == 0
    assert ROW_TILE % RWKV_ROW_TILE == 0 and RWKV_ROW_TILE % t_s == 0

    ffn_w = (ffn_w_gate.astype(BF16), ffn_w_up.astype(BF16), ffn_w_down.astype(BF16))

    x = (x_prompt.reshape(n_p, d), x_sample.reshape(n_s, d))
    new_k, new_v, new_shift, new_wkv = [], [], [], []
    for i in range(depth):
        g = norm_g[i]
        j = i // 2
        x = _ffn_block(x, g[0], g[1], ffn_w, i, 0)
        if i % 2 == 0:
            q, k, v = _qkv_proj(x, g[2], att_w_qkv[j].astype(BF16), att_b_qkv[j], q_dim, kv_dim)

            n_back = WINDOW // CHUNK
            n_c = t_p // CHUNK
            parts = [(k, v, CHUNK, (lambda b, c, back=back: (b * n_c + jnp.maximum(c - back, 0), 0)), back)
                     for back in range(n_back, -1, -1)]
            o_p = _attention(q, parts, _logit_offsets(rel_table, att_sinks[j], CHUNK, (n_back + 1) * CHUNK),
                             grid=(n_bp, n_c), q_rows=CHUNK, q_map=lambda b, c: (b * n_c + c, 0),
                             n_out_rows=n_p)
            first = n_p // t_s
            parts = [(cache_k[j].reshape(n_bs, WINDOW, kv_dim), cache_v[j].reshape(n_bs, WINDOW, kv_dim),
                      WINDOW, (lambda b, c: (b, 0, 0)), 0),
                     (k, v, t_s, (lambda b, c: (first + b, 0)), 0)]
            o_s = _attention(q, parts, _logit_offsets(rel_table, att_sinks[j], t_s, WINDOW + t_s),
                             grid=(n_bs, 1), q_rows=t_s, q_map=lambda b, c: (first + b, 0),
                             n_out_rows=n_s)
            x = _out_proj(o_p, o_s, att_w_o[j].astype(BF16), g[3], x)

            def kv_rows(a):
                newest = jnp.stack([a[(b + 1) * t_p - WINDOW:(b + 1) * t_p] for b in range(n_bp)])
                return (newest.reshape(n_bp, WINDOW, N_KV_HEADS, HEAD_DIM),
                        a[n_p:].reshape(n_bs, t_s, N_KV_HEADS, HEAD_DIM))

            new_k.append(kv_rows(k))
            new_v.append(kv_rows(v))
        else:
            first_rows = jnp.broadcast_to(state_shift[j], (n_bs, t_s, d)).reshape(n_s, d)
            streams = (n_p, t_p, t_s)
            mu = rwkv_mu[j]
            rkv = _rkv_proj(x, g[2], first_rows, streams, jnp.stack([mu[0], mu[2], mu[3]]),
                            jnp.stack([rwkv_w_r[j], rwkv_w_k[j], rwkv_w_v[j]]).astype(BF16))
            w1, w2 = _pad_lora(rwkv_w1[j], rwkv_w2[j])
            a1, a2 = _pad_lora(rwkv_a1[j], rwkv_a2[j])
            g1, g2 = _pad_lora(rwkv_g1[j], rwkv_g2[j])
            lw, al, gate, h_tail = _lora_branches(x, g[2], first_rows, streams,
                                                  jnp.stack([mu[1], mu[4], mu[5]]), rwkv_w0[j], w1, w2,
                                                  rwkv_a0[j], a1, a2, g1, g2)
            vecs = (rwkv_k_k[j], rwkv_k_a[j], rwkv_r_k[j].reshape(d), rwkv_ln_w[j], rwkv_ln_b[j])
            zero_state = jnp.zeros((n_bp, d // RWKV_HEAD, RWKV_HEAD, RWKV_HEAD), F32)
            o_p, wkv_p = _wkv(rkv, lw, al, gate, *vecs, zero_state, n_streams=n_bp,
                              n_chunks=t_p // WKV_CHUNK, rows=WKV_CHUNK, first_block=0)
            o_s, wkv_s = _wkv(rkv, lw, al, gate, *vecs, state_wkv[j], n_streams=n_bs,
                              n_chunks=1, rows=t_s, first_block=n_p // t_s)
            x = _out_proj(o_p, o_s, rwkv_w_o[j].astype(BF16), g[3], x)
            groups = t_p // t_s
            new_shift.append((h_tail[groups - 1:n_p // t_s:groups].reshape(n_bp, 1, d),
                              h_tail[n_p // t_s:].reshape(n_bs, 1, d)))
            new_wkv.append((wkv_p, wkv_s))
        x = _ffn_block(x, g[4], g[5], ffn_w, i, 1, split_rows=n_p if i == depth - 1 else None)

    def both(pairs):
        return jnp.stack([p[0] for p in pairs]), jnp.stack([p[1] for p in pairs])

    k_prompt, k_sample = both(new_k)
    v_prompt, v_sample = both(new_v)
    shift_prompt, shift_sample = both(new_shift)
    wkv_prompt, wkv_sample = both(new_wkv)
    return (x[0].reshape(n_bp, t_p, d), x[1].reshape(n_bs, t_s, d),
            k_prompt, v_prompt, k_sample, v_sample,
            shift_prompt, wkv_prompt, shift_sample, wkv_sample)
```

```python
import functools
import math

import jax
import jax.numpy as jnp
from jax import lax
from jax.experimental import pallas as pl
from jax.experimental.pallas import tpu as pltpu

F32 = jnp.float32
BF16 = jnp.bfloat16

HEAD_DIM = 64
N_KV_HEADS = 4
CHUNK = 64
WINDOW = 128
N_BUCKETS = 32
MAX_DISTANCE = 128
RWKV_HEAD = 64
RMS_EPS = 1e-6
GN_EPS = RWKV_HEAD * 1e-5
NEG_INF = -1e30
LORA_PAD = 128

LANES = 128
KEY_PAD = 256
ATTN_STREAMS_PER_STEP = 2
VMEM_LIMIT_BYTES = 56 * 1024 * 1024
ROW_TILE = 512
RWKV_ROW_TILE = 256
FF_TILE = 512
WKV_CHUNK = 64


def _params(*semantics):
    return pltpu.CompilerParams(dimension_semantics=semantics,
                                vmem_limit_bytes=VMEM_LIMIT_BYTES)


def _rms(x):
    return x * lax.rsqrt(jnp.mean(jnp.square(x), axis=-1, keepdims=True) + RMS_EPS)


def _dot(a, b):
    return jnp.dot(a, b, preferred_element_type=F32)


def _dot_nt(a, b):
    return lax.dot_general(a, b, (((1,), (1,)), ((), ())), preferred_element_type=F32)


def _two_part_specs(d, prompt_tiles):
    return [pl.BlockSpec((ROW_TILE, d), lambda i, *_: (jnp.minimum(i, prompt_tiles - 1), 0)),
            pl.BlockSpec((ROW_TILE, d), lambda i, *_: (jnp.maximum(i - prompt_tiles, 0), 0))]


def _ffn_kernel(*refs, n_x, n_out, prompt_tiles):
    x_refs = refs[:n_x]
    g0_ref, g1_ref, wg_ref, wu_ref, wd_ref = refs[n_x:n_x + 5]
    o_refs = refs[n_x + 5:n_x + 5 + n_out]
    h_ref = refs[n_x + 5 + n_out]
    i = pl.program_id(0)
    f = pl.program_id(1)

    def partial_down(h):
        a = _dot(h, wg_ref[...])
        b = _dot(h, wu_ref[...])
        return _dot((a * jax.nn.sigmoid(a) * b).astype(BF16), wd_ref[...])

    def body(x_ref, o_ref):
        last = pl.num_programs(1) - 1

        @pl.when(f == 0)
        def _():
            h = (_rms(x_ref[...]) * g0_ref[...]).astype(BF16)
            h_ref[...] = h
            o_ref[...] = partial_down(h)

        @pl.when((f > 0) & (f < last))
        def _():
            o_ref[...] += partial_down(h_ref[...])

        @pl.when(f == last)
        def _():
            y = o_ref[...] + partial_down(h_ref[...])
            o_ref[...] = x_ref[...] + 0.5 * (_rms(y) * g1_ref[...])

    if n_x == 1 and n_out == 1:
        body(x_refs[0], o_refs[0])
    else:
        pl.when(i < prompt_tiles)(lambda: body(x_refs[0], o_refs[0]))
        pl.when(i >= prompt_tiles)(lambda: body(x_refs[-1], o_refs[-1]))


def _ffn_block(x, g0, g1, w, layer, half, split_rows=None):
    xs = x if isinstance(x, tuple) else (x,)
    d = xs[0].shape[1]
    n = sum(a.shape[0] for a in xs)
    d_ff = w[0].shape[-1]
    assert d_ff // FF_TILE >= 2
    row = pl.BlockSpec((ROW_TILE, d), lambda i, f: (i, 0))
    prompt_tiles = n // ROW_TILE
    x_specs, out_specs, out_shape = [row], [row], [jax.ShapeDtypeStruct((n, d), F32)]
    if len(xs) == 2:
        prompt_tiles = xs[0].shape[0] // ROW_TILE
        x_specs = _two_part_specs(d, prompt_tiles)
    if split_rows is not None:
        prompt_tiles = split_rows // ROW_TILE
        out_specs = _two_part_specs(d, prompt_tiles)
        out_shape = [jax.ShapeDtypeStruct((split_rows, d), F32),
                     jax.ShapeDtypeStruct((n - split_rows, d), F32)]
    out = pl.pallas_call(
        functools.partial(_ffn_kernel, n_x=len(xs), n_out=len(out_shape), prompt_tiles=prompt_tiles),
        grid=(n // ROW_TILE, d_ff // FF_TILE),
        in_specs=x_specs + [
            pl.BlockSpec((1, d), lambda i, f: (0, 0)),
            pl.BlockSpec((1, d), lambda i, f: (0, 0)),
            pl.BlockSpec((None, None, d, FF_TILE), lambda i, f: (layer, half, 0, f)),
            pl.BlockSpec((None, None, d, FF_TILE), lambda i, f: (layer, half, 0, f)),
            pl.BlockSpec((None, None, FF_TILE, d), lambda i, f: (layer, half, f, 0)),
        ],
        out_specs=out_specs,
        out_shape=out_shape,
        scratch_shapes=[pltpu.VMEM((ROW_TILE, d), BF16)],
        compiler_params=_params("arbitrary", "arbitrary"),
        name="ffn_block",
    )(*xs, g0.reshape(1, d), g1.reshape(1, d), *w)
    return tuple(out) if split_rows is not None else out[0]


def _normed_and_shifted(x_ref, g_ref, first_ref, carry_ref, streams):
    prompt_tiles, prompt_len, sample_len = streams
    tile = x_ref.shape[0]
    i = pl.program_id(0)

    @pl.when(i == 0)
    def _():
        carry_ref[...] = jnp.zeros_like(carry_ref)

    h = _rms(x_ref[...]) * g_ref[...]
    rows = lax.broadcasted_iota(jnp.int32, (tile, 1), 0)
    prev = jnp.where(rows == 0, carry_ref[...], pltpu.roll(h, 1, axis=0))
    carry_ref[...] = h[tile - 1:tile]
    prompt_start = (rows + i * tile) % prompt_len == 0
    sample_start = rows % sample_len == 0
    return h, jnp.where(i < prompt_tiles, jnp.where(prompt_start, 0.0, prev),
                        jnp.where(sample_start, first_ref[...], prev))


def _shift_specs(d, tile, n_prompt_rows):
    prompt_tiles = n_prompt_rows // tile
    return [pl.BlockSpec((tile, d), lambda i: (i, 0)),
            pl.BlockSpec((1, d), lambda i: (0, 0)),
            pl.BlockSpec((tile, d), lambda i: (jnp.maximum(i - prompt_tiles, 0), 0))]


def _qkv_kernel(x_ref, g_ref, w_ref, b_ref, q_ref, k_ref, v_ref):
    h = (_rms(x_ref[...]) * g_ref[...]).astype(BF16)
    y = _dot(h, w_ref[...]) + b_ref[...]
    q_dim = q_ref.shape[1]
    kv_dim = k_ref.shape[1]
    q_ref[...] = y[:, :q_dim].astype(q_ref.dtype)
    k_ref[...] = y[:, q_dim:q_dim + kv_dim]
    v_ref[...] = y[:, q_dim + kv_dim:]


def _qkv_proj(x, g, w, b, q_dim, kv_dim):
    n, d = x.shape
    m = w.shape[1]

    def rows(width):
        return pl.BlockSpec((ROW_TILE, width), lambda i: (i, 0))

    return pl.pallas_call(
        _qkv_kernel,
        grid=(n // ROW_TILE,),
        in_specs=[rows(d), pl.BlockSpec((1, d), lambda i: (0, 0)),
                  pl.BlockSpec((d, m), lambda i: (0, 0)), pl.BlockSpec((1, m), lambda i: (0, 0))],
        out_specs=[rows(q_dim), rows(kv_dim), rows(kv_dim)],
        out_shape=[jax.ShapeDtypeStruct((n, q_dim), BF16), jax.ShapeDtypeStruct((n, kv_dim), F32),
                   jax.ShapeDtypeStruct((n, kv_dim), F32)],
        compiler_params=_params("parallel"),
        name="qkv_proj",
    )(x, g.reshape(1, d), w, b.reshape(1, m))


def _rkv_kernel(x_ref, g_ref, first_ref, mu_ref, w_ref, o_ref, carry_ref, *, streams):
    h, hp = _normed_and_shifted(x_ref, g_ref, first_ref, carry_ref, streams)
    dx = hp - h
    for s in range(w_ref.shape[0]):
        o_ref[s] = _dot((h + dx * mu_ref[s]).astype(BF16), w_ref[s])


def _rkv_proj(x, g, first_rows, streams, mu, w):
    n, d = x.shape
    k = w.shape[0]
    m = w.shape[2]
    n_prompt_rows, prompt_len, sample_len = streams
    return pl.pallas_call(
        functools.partial(_rkv_kernel, streams=(n_prompt_rows // RWKV_ROW_TILE, prompt_len, sample_len)),
        grid=(n // RWKV_ROW_TILE,),
        in_specs=_shift_specs(d, RWKV_ROW_TILE, n_prompt_rows) + [
            pl.BlockSpec((k, 1, d), lambda i: (0, 0, 0)),
            pl.BlockSpec((k, d, m), lambda i: (0, 0, 0), pipeline_mode=pl.Buffered(1))],
        out_specs=pl.BlockSpec((k, RWKV_ROW_TILE, m), lambda i: (0, i, 0)),
        out_shape=jax.ShapeDtypeStruct((k, n, m), F32),
        scratch_shapes=[pltpu.VMEM((1, d), F32)],
        compiler_params=_params("arbitrary"),
        name="rkv_proj",
    )(x, g.reshape(1, d), first_rows, mu.reshape(k, 1, d), w)


def _lora_kernel(x_ref, g_ref, first_ref, mu_ref, w0_ref, w1_ref, w2_ref, a0_ref, a1_ref, a2_ref,
                 g1_ref, g2_ref, lw_ref, al_ref, gate_ref, tail_ref, carry_ref, *, streams):
    h, hp = _normed_and_shifted(x_ref, g_ref, first_ref, carry_ref, streams)
    dx = hp - h
    sample_len = streams[2]
    tail_ref[...] = jnp.concatenate(
        [h[(t + 1) * sample_len - 1:(t + 1) * sample_len] for t in range(tail_ref.shape[0])], axis=0)

    xw = (h + dx * mu_ref[0]).astype(BF16)
    t = jnp.tanh(_dot(xw, w1_ref[...])).astype(BF16)
    u_w = w0_ref[...] + _dot(t, w2_ref[...])
    lw_ref[...] = -math.exp(-0.5) * jax.nn.sigmoid(u_w)

    xa = (h + dx * mu_ref[1]).astype(BF16)
    u = _dot(xa, a1_ref[...]).astype(BF16)
    al_ref[...] = jax.nn.sigmoid(a0_ref[...] + _dot(u, a2_ref[...]))

    xg = (h + dx * mu_ref[2]).astype(BF16)
    s = jax.nn.sigmoid(_dot(xg, g1_ref[...])).astype(BF16)
    gate_ref[...] = _dot(s, g2_ref[...])


def _lora_branches(x, g, first_rows, streams, mu, w0, w1, w2, a0, a1, a2, g1, g2):
    n, d = x.shape
    n_prompt_rows, prompt_len, sample_len = streams
    tile = RWKV_ROW_TILE
    row = pl.BlockSpec((tile, d), lambda i: (i, 0))
    tail = pl.BlockSpec((tile // sample_len, d), lambda i: (i, 0))

    def whole(a):
        return pl.BlockSpec(a.shape, lambda i: (0,) * a.ndim)

    mu = mu.reshape(3, 1, d)
    w0 = w0.reshape(1, d)
    a0 = a0.reshape(1, d)
    consts = (mu, w0, w1, w2, a0, a1, a2, g1, g2)
    out = jax.ShapeDtypeStruct((n, d), F32)
    return pl.pallas_call(
        functools.partial(_lora_kernel, streams=(n_prompt_rows // tile, prompt_len, sample_len)),
        grid=(n // tile,),
        in_specs=_shift_specs(d, tile, n_prompt_rows) + [whole(c) for c in consts],
        out_specs=[row, row, row, tail],
        out_shape=[out, out, out, jax.ShapeDtypeStruct((n // sample_len, d), F32)],
        scratch_shapes=[pltpu.VMEM((1, d), F32)],
        compiler_params=_params("arbitrary"),
        name="rwkv_lora",
    )(x, g.reshape(1, d), first_rows, *consts)


def _out_proj_kernel(op_ref, os_ref, w_ref, g_ref, x_ref, y_ref, *, prompt_tiles):
    o = jnp.where(pl.program_id(0) < prompt_tiles, op_ref[...], os_ref[...])
    y = _dot(o, w_ref[...])
    y_ref[...] = x_ref[...] + _rms(y) * g_ref[...]


def _out_proj(o_prompt, o_sample, w, g, xres):
    n, d = xres.shape
    k = o_prompt.shape[1]
    prompt_tiles = o_prompt.shape[0] // ROW_TILE
    return pl.pallas_call(
        functools.partial(_out_proj_kernel, prompt_tiles=prompt_tiles),
        grid=(n // ROW_TILE,),
        in_specs=_two_part_specs(k, prompt_tiles) + [
            pl.BlockSpec((k, d), lambda i: (0, 0)),
            pl.BlockSpec((1, d), lambda i: (0, 0)),
            pl.BlockSpec((ROW_TILE, d), lambda i: (i, 0))],
        out_specs=pl.BlockSpec((ROW_TILE, d), lambda i: (i, 0)),
        out_shape=jax.ShapeDtypeStruct((n, d), F32),
        compiler_params=_params("arbitrary"),
        name="out_proj",
    )(o_prompt, o_sample, w, g.reshape(1, d), xres)


def _t5_bucket(rel):
    nb = N_BUCKETS // 2
    max_exact = nb // 2
    offset = jnp.where(rel > 0, nb, 0)
    n = jnp.abs(rel)
    nf = jnp.maximum(n, 1).astype(F32)
    large = max_exact + (jnp.log(nf / max_exact) / math.log(MAX_DISTANCE / max_exact)
                         * (nb - max_exact)).astype(jnp.int32)
    large = jnp.minimum(large, nb - 1)
    return offset + jnp.where(n < max_exact, n, large)


def _bias_kernel(tt_ref, b_ref, o_ref):
    onehot = (lax.broadcasted_iota(jnp.int32, (tt_ref.shape[1], b_ref.shape[1]), 0) == b_ref[...]).astype(F32)
    o_ref[...] = jnp.dot(tt_ref[...], onehot, preferred_element_type=F32,
                         precision=lax.Precision.HIGHEST)


def _logit_offsets(table, sinks, n_q, n_k):
    assert n_k < KEY_PAD
    rel = (jnp.arange(n_k, dtype=jnp.int32)[None, :] - WINDOW - jnp.arange(n_q, dtype=jnp.int32)[:, None])
    rows = jnp.concatenate([_t5_bucket(rel).astype(jnp.int32),
                            jnp.full((n_q, 1), N_BUCKETS, jnp.int32),
                            jnp.full((n_q, KEY_PAD - n_k - 1), N_BUCKETS + 1, jnp.int32)], axis=1)
    n_heads = table.shape[1]
    values = jnp.concatenate([table.T, sinks.astype(F32)[:, None],
                              jnp.full((n_heads, 1), NEG_INF, F32)], axis=1)
    out = pl.pallas_call(
        _bias_kernel,
        out_shape=jax.ShapeDtypeStruct((n_heads, n_q * KEY_PAD), F32),
        name="logit_offsets",
    )(values, rows.reshape(1, n_q * KEY_PAD))
    return out.reshape(n_heads, n_q, KEY_PAD)


def _attn_kernel(*refs, n_parts, group, first_valid_step):
    n_s = ATTN_STREAMS_PER_STEP
    q_refs = refs[:n_s]
    k_refs = [refs[n_s + s * n_parts:n_s + (s + 1) * n_parts] for s in range(n_s)]
    v_refs = [refs[n_s * (1 + n_parts) + s * n_parts:n_s * (1 + n_parts) + (s + 1) * n_parts]
              for s in range(n_s)]
    off_ref, o_ref = refs[n_s * (1 + 2 * n_parts):]
    n_q = q_refs[0].shape[0]
    step = pl.program_id(1)
    scale = HEAD_DIM ** -0.5
    part_rows = [r.shape[0] for r in k_refs[0]]
    n_k = sum(part_rows)
    units = [(s, hk) for s in range(n_s) for hk in range(N_KV_HEADS)]
    tiles = range(group // 2)

    col = lax.broadcasted_iota(jnp.int32, (1, KEY_PAD), 1)
    hidden = None
    start = 0
    for p, rows in enumerate(part_rows):
        if first_valid_step[p] > 0:
            h_p = (col >= start) & (col < start + rows) & (step < first_valid_step[p])
            hidden = h_p if hidden is None else hidden | h_p
        start += rows

    zeros = jnp.zeros((KEY_PAD, HEAD_DIM), F32)
    ones = jnp.ones((KEY_PAD, HEAD_DIM), F32)

    def head_rows(refs_, hk):
        kv_cols = slice(hk * HEAD_DIM, (hk + 1) * HEAD_DIM)
        return jnp.concatenate([r[:, kv_cols] for r in refs_] + [zeros[:KEY_PAD - n_k]], axis=0)

    scores = []
    for s, hk in units:
        k = head_rows(k_refs[s], hk)
        k2 = jnp.concatenate([jnp.concatenate([k, zeros], axis=1),
                              jnp.concatenate([zeros, k], axis=1)], axis=0).astype(BF16)
        q_rows = jnp.concatenate(
            [q_refs[s][:, (hk * group + 2 * t) * HEAD_DIM:(hk * group + 2 * t + 2) * HEAD_DIM] for t in tiles],
            axis=0)
        off = jnp.concatenate(
            [jnp.concatenate([off_ref[hk * group + 2 * t], off_ref[hk * group + 2 * t + 1]], axis=1)
             for t in tiles], axis=0)
        sc = _dot_nt(q_rows, k2) * scale + off
        if hidden is not None:
            sc = jnp.where(jnp.concatenate([hidden, hidden], axis=1), NEG_INF, sc)
        scores.append(sc)

    expo = []
    for sc in scores:
        m0 = jnp.max(sc[:, :KEY_PAD], axis=-1, keepdims=True)
        m1 = jnp.max(sc[:, KEY_PAD:], axis=-1, keepdims=True)
        expo.append(jnp.concatenate([jnp.exp(sc[:, :KEY_PAD] - m0), jnp.exp(sc[:, KEY_PAD:] - m1)],
                                    axis=1).astype(BF16))

    for (s, hk), e in zip(units, expo):
        v = head_rows(v_refs[s], hk)
        rhs = jnp.concatenate([jnp.concatenate([v, zeros, ones, zeros], axis=1),
                               jnp.concatenate([zeros, v, zeros, ones], axis=1)], axis=0).astype(BF16)
        acc = _dot(e, rhs)
        o = acc[:, :LANES] / acc[:, LANES:]
        for t in tiles:
            first_head = hk * group + 2 * t
            o_ref[s, :, first_head * HEAD_DIM:(first_head + 2) * HEAD_DIM] = (
                o[t * n_q:(t + 1) * n_q].astype(o_ref.dtype))


def _attention(q, parts, offsets, *, n_streams, n_steps, q_rows, q_map):
    q_dim = q.shape[1]
    n_heads = q_dim // HEAD_DIM
    group = n_heads // N_KV_HEADS
    per_step = ATTN_STREAMS_PER_STEP
    assert offsets.shape == (n_heads, q_rows, KEY_PAD) and group % 2 == 0 and n_streams % per_step == 0

    def of_stream(imap, s):
        return lambda b, c: imap(per_step * b + s, c)

    def kv_spec(arr, rows, imap):
        if arr.ndim == 3:
            return pl.BlockSpec((None, rows, arr.shape[2]), imap)
        return pl.BlockSpec((rows, arr.shape[1]), imap)

    streams = range(per_step)
    q_specs = [pl.BlockSpec((q_rows, q_dim), of_stream(q_map, s)) for s in streams]
    k_specs = [kv_spec(p[0], p[2], of_stream(p[3], s)) for s in streams for p in parts]
    v_specs = [kv_spec(p[1], p[2], of_stream(p[3], s)) for s in streams for p in parts]
    out = pl.pallas_call(
        functools.partial(_attn_kernel, n_parts=len(parts), group=group,
                          first_valid_step=tuple(p[4] for p in parts)),
        grid=(n_streams // per_step, n_steps),
        in_specs=q_specs + k_specs + v_specs + [pl.BlockSpec(offsets.shape, lambda b, c: (0, 0, 0))],
        out_specs=pl.BlockSpec((None, per_step, q_rows, q_dim), lambda b, c: (b, 0, c, 0)),
        out_shape=jax.ShapeDtypeStruct((n_streams // per_step, per_step, n_steps * q_rows, q_dim), BF16),
        compiler_params=_params("parallel", "arbitrary"),
        name="swa_attention",
    )(*[q] * per_step, *[p[0] for p in parts] * per_step, *[p[1] for p in parts] * per_step, offsets)
    return out.reshape(n_streams * n_steps * q_rows, q_dim)


def _split3(x):
    hi = x.astype(BF16)
    r1 = x - hi.astype(F32)
    mid = r1.astype(BF16)
    lo = (r1 - mid.astype(F32)).astype(BF16)
    return hi, mid, lo


def _wkv_kernel(r_ref, k_ref, v_ref, lw_ref, al_ref, gate_ref, kk_ref, ka_ref, rk_ref, lnw_ref,
                lnb_ref, s0_ref, o_ref, s_out_ref, h_ref):
    valid = r_ref.shape[0]
    L = WKV_CHUNK
    while L // 2 >= max(valid, 16):
        L //= 2
    L2 = 2 * L
    n_pairs = r_ref.shape[1] // LANES
    c_idx = pl.program_id(1)
    half = RWKV_HEAD

    lane = lax.broadcasted_iota(jnp.int32, (1, LANES), 1)
    head0 = lane < half
    row2 = lax.broadcasted_iota(jnp.int32, (L2, L2), 0)
    col2 = lax.broadcasted_iota(jnp.int32, (L2, L2), 1)
    strict_lower = row2 > col2
    lower = row2 >= col2
    eye2 = (row2 == col2).astype(F32)
    tril = (lax.broadcasted_iota(jnp.int32, (L, L), 0) >= lax.broadcasted_iota(jnp.int32, (L, L), 1)
            ).astype(BF16)
    krow = lax.broadcasted_iota(jnp.int32, (LANES, LANES), 0)
    kcol = lax.broadcasted_iota(jnp.int32, (LANES, LANES), 1)
    same_head = (krow < half) == (kcol < half)
    eye_k = (krow == kcol).astype(F32)

    def load(ref, cs):
        x = ref[:, cs]
        if valid < L:
            x = jnp.concatenate([x, jnp.zeros((L - valid, LANES), x.dtype)], axis=0)
        return x

    def seg_sum(x):
        s0 = jnp.sum(jnp.where(head0, x, 0.0), axis=-1, keepdims=True)
        s1 = jnp.sum(jnp.where(head0, 0.0, x), axis=-1, keepdims=True)
        return jnp.where(head0, s0, s1)

    def stack2(x):
        return jnp.concatenate([jnp.where(head0, x, 0.0), jnp.where(head0, 0.0, x)], axis=0)

    def fold(m):
        return m[:L] + m[L:]

    @pl.when(c_idx == 0)
    def _():
        for p in range(n_pairs):
            s0 = jnp.concatenate([s0_ref[2 * p].T, s0_ref[2 * p + 1].T], axis=0)
            h_ref[p] = jnp.where(same_head, jnp.concatenate([s0, s0], axis=1), 0.0)

    cols =[slice(p * LANES, (p + 1) * LANES) for p in range(n_pairs)]
    val = [dict() for _ in range(n_pairs)]

    def s_load(ps):
        for p in ps:
            t = val[p]
            t["r"], t["v"], t["lw"] = load(r_ref, cols[p]), load(v_ref, cols[p]), load(lw_ref, cols[p])
            c3 = _dot(tril, jnp.concatenate(_split3(t["lw"]), axis=1))
            t["c"] = c3[:, :LANES] + c3[:, LANES:2 * LANES] + c3[:, 2 * LANES:]

    def s_keys(ps):
        for p in ps:
            t = val[p]
            k_raw = load(k_ref, cols[p])
            al = load(al_ref, cols[p])
            kk = k_raw * kk_ref[:, cols[p]]
            kk = kk / jnp.maximum(jnp.sqrt(seg_sum(kk * kk)), 1e-12)
            t["k"] = k_raw * (1.0 + (al - 1.0) * ka_ref[:, cols[p]])
            t["a"] = -kk
            t["b"] = kk * al

    def s_gram(ps):
        for p in ps:
            t = val[p]
            c = t["c"]
            c_last = c[L - 1:L]
            e_neg = jnp.exp(-c)
            e_end = jnp.exp(c_last - c)
            t["at2"] = stack2(t["a"] * jnp.exp(c - t["lw"]))
            t["rt"] = t["r"] * jnp.exp(c)
            t["v2"] = stack2(t["v"])
            t["bc2"] = stack2(t["b"] * e_end)
            t["kc2"] = stack2(t["k"] * e_end)
            t["d_end"] = jnp.exp(c_last)
            t["gram"] = _dot_nt(
                jnp.concatenate([t["at2"], stack2(t["rt"])], axis=0).astype(BF16),
                jnp.concatenate([stack2(t["b"] * e_neg), stack2(t["k"] * e_neg)], axis=0).astype(BF16))

    def s_masks(ps):
        for p in ps:
            t = val[p]
            g = t.pop("gram")
            t["n_mat"] = jnp.where(strict_lower, g[:L2, :L2], 0.0)
            t["akv2"] = _dot(jnp.where(strict_lower, g[:L2, L2:], 0.0).astype(BF16), t["v2"].astype(BF16))
            t["lhs_y"] = jnp.concatenate([fold(jnp.where(lower, g[L2:, :L2], 0.0)),
                                          fold(jnp.where(lower, g[L2:, L2:], 0.0))], axis=1)
            quadrant = (row2 // 2 == col2 // 2) & (row2 % 2 == 1) & (col2 % 2 == 0)
            t["t_inv"] = eye2 + jnp.where(quadrant, t["n_mat"], 0.0)

    def s_level(size):
        def stage(ps):
            quadrant = ((row2 // (2 * size)) == (col2 // (2 * size))) & \
                       ((row2 // size) % 2 == 1) & ((col2 // size) % 2 == 0)
            t_bf = {p: val[p]["t_inv"].astype(BF16) for p in ps}
            tn = {p: _dot(t_bf[p], jnp.where(quadrant, val[p]["n_mat"], 0.0).astype(BF16)).astype(BF16)
                  for p in ps}
            for p in ps:
                val[p]["t_inv"] = val[p]["t_inv"] + _dot(tn[p], t_bf[p])
        return stage

    def s_solve(ps):
        for p in ps:
            t = val[p]
            t["u2"] = _dot(t.pop("t_inv").astype(BF16),
                           jnp.concatenate([t["at2"], t.pop("akv2")], axis=1).astype(BF16))

    def s_yz(ps):
        for p in ps:
            t = val[p]
            z = jnp.concatenate(
                [t.pop("u2"), jnp.concatenate([jnp.zeros((L2, LANES), F32), t["v2"]], axis=1)],
                axis=0).astype(BF16)
            lhs_h = jnp.concatenate([t["bc2"], t["kc2"]], axis=0).T
            t["yz"] = _dot(jnp.concatenate([t["lhs_y"], lhs_h], axis=0).astype(BF16), z)

    def s_state(ps):
        for p in ps:
            t = val[p]
            q_hat = t["rt"] + t["yz"][:L, :LANES]
            m_mat = eye_k * t["d_end"] + t["yz"][L:, :LANES]
            t["hh"] = _dot(jnp.concatenate([m_mat, q_hat], axis=0).astype(BF16), h_ref[p].astype(BF16))

    def s_out(ps):
        for p in ps:
            t = val[p]
            h_ref[p] = jnp.where(same_head, t["hh"][:LANES] + t["yz"][L:, LANES:], 0.0)
            y = t["hh"][LANES:] + t["yz"][:L, LANES:]
            mean = seg_sum(y) * (1.0 / half)
            yc = y - mean
            var = seg_sum(yc * yc) * (1.0 / half)
            yn = yc * lax.rsqrt(var + GN_EPS) * lnw_ref[:, cols[p]] + lnb_ref[:, cols[p]]
            bonus = seg_sum(t["r"] * t["k"] * rk_ref[:, cols[p]]) * t["v"]
            out = yn + bonus
            o_ref[:, cols[p]] = (out[:valid] * gate_ref[:, cols[p]]).astype(o_ref.dtype)
            t.clear()

    sizes = []
    size = 2
    while size < L:
        sizes.append(size)
        size *= 2
    for stage in [s_load, s_keys, s_gram, s_masks] + [s_level(sz) for sz in sizes] + [s_solve, s_yz, s_state, s_out]:
        stage(range(n_pairs))

    @pl.when(c_idx == pl.num_programs(1) - 1)
    def _():
        for p in range(n_pairs):
            h_last = h_ref[p]
            s_out_ref[2 * p] = h_last[:half, :half].T
            s_out_ref[2 * p + 1] = h_last[half:, half:].T


def _wkv(rkv, lw, al, gate, k_k, k_a, r_k, ln_w, ln_b, state, *, n_streams, n_chunks, rows, first_block):
    d = lw.shape[1]
    n_pairs = d // LANES

    def block(b, c):
        return first_block + b * n_chunks + c

    seq = pl.BlockSpec((rows, d), lambda b, c: (block(b, c), 0))
    rkv_specs = [pl.BlockSpec((None, rows, d), lambda b, c, s=s: (s, block(b, c), 0)) for s in range(3)]
    vec = pl.BlockSpec((1, d), lambda b, c: (0, 0))
    st = pl.BlockSpec((None,) + state.shape[1:], lambda b, c: (b, 0, 0, 0))
    return pl.pallas_call(
        _wkv_kernel,
        grid=(n_streams, n_chunks),
        in_specs=rkv_specs + [seq] * 3 + [vec] * 5 + [st],
        out_specs=[pl.BlockSpec((rows, d), lambda b, c: (b * n_chunks + c, 0)), st],
        out_shape=[jax.ShapeDtypeStruct((n_streams * n_chunks * rows, d), BF16),
                   jax.ShapeDtypeStruct(state.shape, F32)],
        scratch_shapes=[pltpu.VMEM((n_pairs, LANES, LANES), F32)],
        compiler_params=_params("parallel", "arbitrary"),
        name="rwkv7_wkv",
    )(rkv, rkv, rkv, lw, al, gate, *[x.reshape(1, d) for x in (k_k, k_a, r_k, ln_w, ln_b)], state)


def _pad_lora(w_in, w_out):
    rank = w_in.shape[1]
    pad = (-rank) % LORA_PAD
    return (jnp.pad(w_in, ((0, 0), (0, pad))).astype(BF16),
            jnp.pad(w_out, ((0, pad), (0, 0))).astype(BF16))


def kernel(x_prompt, x_sample, cache_k, cache_v, state_shift, state_wkv, norm_g, ffn_w_gate, ffn_w_up, ffn_w_down, rel_table, att_w_qkv, att_b_qkv, att_w_o, att_sinks, rwkv_mu, rwkv_w_r, rwkv_w_k, rwkv_w_v, rwkv_w_o, rwkv_w0, rwkv_w1, rwkv_w2, rwkv_a0, rwkv_a1, rwkv_a2, rwkv_g1, rwkv_g2, rwkv_k_k, rwkv_k_a, rwkv_r_k, rwkv_ln_w, rwkv_ln_b):
    n_bp, t_p, d = x_prompt.shape
    n_bs, t_s, _ = x_sample.shape
    n_p = n_bp * t_p
    n_s = n_bs * t_s
    depth = norm_g.shape[0]
    q_dim = att_w_o.shape[1]
    kv_dim = N_KV_HEADS * HEAD_DIM
    assert n_p % ROW_TILE == 0 and n_s % ROW_TILE == 0 and ROW_TILE % t_s == 0 and t_p % CHUNK == 0
    assert t_p % WKV_CHUNK == 0 and t_s <= WKV_CHUNK and t_p % t_s == 0
    assert ROW_TILE % RWKV_ROW_TILE == 0 and RWKV_ROW_TILE % t_s == 0

    ffn_w = (ffn_w_gate.astype(BF16), ffn_w_up.astype(BF16), ffn_w_down.astype(BF16))

    x = (x_prompt.reshape(n_p, d), x_sample.reshape(n_s, d))
    new_k, new_v, new_shift, new_wkv = [], [], [], []
    for i in range(depth):
        g = norm_g[i]
        j = i // 2
        x = _ffn_block(x, g[0], g[1], ffn_w, i, 0)
        if i % 2 == 0:
            q, k, v = _qkv_proj(x, g[2], att_w_qkv[j].astype(BF16), att_b_qkv[j], q_dim, kv_dim)

            n_back = WINDOW // CHUNK
            n_c = t_p // CHUNK
            parts = [(k, v, CHUNK, (lambda b, c, back=back: (b * n_c + jnp.maximum(c - back, 0), 0)), back)
                     for back in range(n_back, -1, -1)]
            o_p = _attention(q, parts, _logit_offsets(rel_table, att_sinks[j], CHUNK, (n_back + 1) * CHUNK),
                             n_streams=n_bp, n_steps=n_c, q_rows=CHUNK, q_map=lambda b, c: (b * n_c + c, 0))
            first = n_p // t_s
            parts = [(cache_k[j].reshape(n_bs, WINDOW, kv_dim), cache_v[j].reshape(n_bs, WINDOW, kv_dim),
                      WINDOW, (lambda b, c: (b, 0, 0)), 0),
                     (k, v, t_s, (lambda b, c: (first + b, 0)), 0)]
            o_s = _attention(q, parts, _logit_offsets(rel_table, att_sinks[j], t_s, WINDOW + t_s),
                             n_streams=n_bs, n_steps=1, q_rows=t_s, q_map=lambda b, c: (first + b, 0))
            x = _out_proj(o_p, o_s, att_w_o[j].astype(BF16), g[3], x)

            def kv_rows(a):
                newest = jnp.stack([a[(b + 1) * t_p - WINDOW:(b + 1) * t_p] for b in range(n_bp)])
                return (newest.reshape(n_bp, WINDOW, N_KV_HEADS, HEAD_DIM),
                        a[n_p:].reshape(n_bs, t_s, N_KV_HEADS, HEAD_DIM))

            new_k.append(kv_rows(k))
            new_v.append(kv_rows(v))
        else:
            first_rows = jnp.broadcast_to(state_shift[j], (n_bs, t_s, d)).reshape(n_s, d)
            streams = (n_p, t_p, t_s)
            mu = rwkv_mu[j]
            rkv = _rkv_proj(x, g[2], first_rows, streams, jnp.stack([mu[0], mu[2], mu[3]]),
                            jnp.stack([rwkv_w_r[j], rwkv_w_k[j], rwkv_w_v[j]]).astype(BF16))
            w1, w2 = _pad_lora(rwkv_w1[j], rwkv_w2[j])
            a1, a2 = _pad_lora(rwkv_a1[j], rwkv_a2[j])
            g1, g2 = _pad_lora(rwkv_g1[j], rwkv_g2[j])
            lw, al, gate, h_tail = _lora_branches(x, g[2], first_rows, streams,
                                                  jnp.stack([mu[1], mu[4], mu[5]]), rwkv_w0[j], w1, w2,
                                                  rwkv_a0[j], a1, a2, g1, g2)
            vecs = (rwkv_k_k[j], rwkv_k_a[j], rwkv_r_k[j].reshape(d), rwkv_ln_w[j], rwkv_ln_b[j])
            zero_state = jnp.zeros((n_bp, d // RWKV_HEAD, RWKV_HEAD, RWKV_HEAD), F32)
            o_p, wkv_p = _wkv(rkv, lw, al, gate, *vecs, zero_state, n_streams=n_bp,
                              n_chunks=t_p // WKV_CHUNK, rows=WKV_CHUNK, first_block=0)
            o_s, wkv_s = _wkv(rkv, lw, al, gate, *vecs, state_wkv[j], n_streams=n_bs,
                              n_chunks=1, rows=t_s, first_block=n_p // t_s)
            x = _out_proj(o_p, o_s, rwkv_w_o[j].astype(BF16), g[3], x)
            groups = t_p // t_s
            new_shift.append((h_tail[groups - 1:n_p // t_s:groups].reshape(n_bp, 1, d),
                              h_tail[n_p // t_s:].reshape(n_bs, 1, d)))
            new_wkv.append((wkv_p, wkv_s))
        x = _ffn_block(x, g[4], g[5], ffn_w, i, 1, split_rows=n_p if i == depth - 1 else None)

    def both(pairs):
        return jnp.stack([p[0] for p in pairs]), jnp.stack([p[1] for p in pairs])

    k_prompt, k_sample = both(new_k)
    v_prompt, v_sample = both(new_v)
    shift_prompt, shift_sample = both(new_shift)
    wkv_prompt, wkv_sample = both(new_wkv)
    return (x[0].reshape(n_bp, t_p, d), x[1].reshape(n_bs, t_s, d),
            k_prompt, v_prompt, k_sample, v_sample,
            shift_prompt, wkv_prompt, shift_sample, wkv_sample)
```
